```python
import math
import jax, jax.numpy as jnp
from jax import lax
import numpy as np

D_MODEL = 1024
BATCH = 8
SEQ = 4096
DEPTH = 1

CHUNK = 64
Q_BLOCK = 128
N_HEADS = 8
QK_NOPE_DIM = 64
QK_ROPE_DIM = 32
V_HEAD_DIM = 64
Q_LORA_RANK = 512
KV_LORA_RANK = 256
MLA_WIDTH = N_HEADS * V_HEAD_DIM
ROPE_THETA = 10000.0
POOL_WINDOWS = (2, 4, 8, 16)
N_POOL_GROUPS = len(POOL_WINDOWS)
POOL_WIDTH = D_MODEL // 2
POOL_GROUP_DIM = POOL_WIDTH // N_POOL_GROUPS
MIX_WIDTH = MLA_WIDTH + POOL_WIDTH
IN_COLS = Q_LORA_RANK + KV_LORA_RANK + QK_ROPE_DIM + POOL_WIDTH
D_FF = 2816
LN_EPS = 1e-5
RMS_EPS = 1e-6
N_SUBLAYERS = 3
ALPHA = (2.0 * DEPTH) ** 0.25
BETA = (8.0 * DEPTH) ** -0.25

kernel_name = "hybrid_mla_multiscale_pool_macaron_deepnorm_adaln"


def _layernorm(x, g, b):
    xf = x.astype(jnp.float32)
    mu = jnp.mean(xf, axis=-1, keepdims=True)
    var = jnp.mean(jnp.square(xf - mu), axis=-1, keepdims=True)
    return ((xf - mu) * lax.rsqrt(var + LN_EPS) * g.astype(jnp.float32) + b.astype(jnp.float32)).astype(x.dtype)


def _rmsnorm(x, g):
    xf = x.astype(jnp.float32)
    ms = jnp.mean(jnp.square(xf), axis=-1, keepdims=True)
    return (xf * lax.rsqrt(ms + RMS_EPS) * g.astype(jnp.float32)).astype(x.dtype)


def _rope(x, cos, sin):
    x1, x2 = jnp.split(x, 2, axis=-1)
    return jnp.concatenate([x1 * cos - x2 * sin, x1 * sin + x2 * cos], axis=-1)


def _swiglu(h, w_in, w_out):
    g, u = jnp.split(h @ w_in, 2, axis=-1)
    return (jax.nn.silu(g) * u) @ w_out


def _mla(cq, ckv, k_rope_raw, positions, q_norm_g, kv_norm_g, w_uq, w_ukv):
    B, S, _ = cq.shape
    q = (_rmsnorm(cq, q_norm_g) @ w_uq).reshape(B, S, N_HEADS, QK_NOPE_DIM + QK_ROPE_DIM)
    kv = (_rmsnorm(ckv, kv_norm_g) @ w_ukv).reshape(B, S, N_HEADS, QK_NOPE_DIM + V_HEAD_DIM)
    q_nope, q_rope = q[..., :QK_NOPE_DIM], q[..., QK_NOPE_DIM:]
    k_nope, v = kv[..., :QK_NOPE_DIM], kv[..., QK_NOPE_DIM:]
    inv_freq = ROPE_THETA ** (-jnp.arange(0, QK_ROPE_DIM, 2, dtype=jnp.float32) / QK_ROPE_DIM)
    ang = positions.astype(jnp.float32)[..., None] * inv_freq
    cos = jnp.cos(ang).astype(cq.dtype)
    sin = jnp.sin(ang).astype(cq.dtype)
    q_rope = _rope(q_rope, cos[:, :, None, :], sin[:, :, None, :])
    k_rope = _rope(k_rope_raw, cos, sin)
    sm_scale = (QK_NOPE_DIM + QK_ROPE_DIM) ** -0.5
    key_chunk = jnp.arange(S) // CHUNK
    n_blocks = S // Q_BLOCK

    def attend_block(i):
        start = i * Q_BLOCK
        qn = lax.dynamic_slice_in_dim(q_nope, start, Q_BLOCK, axis=1)
        qr = lax.dynamic_slice_in_dim(q_rope, start, Q_BLOCK, axis=1)
        s = (jnp.einsum('bqhd,bkhd->bhqk', qn, k_nope)
             + jnp.einsum('bqhr,bkr->bhqk', qr, k_rope)).astype(jnp.float32) * sm_scale
        q_chunk = (start + jnp.arange(Q_BLOCK)) // CHUNK
        mask = key_chunk[None, :] <= q_chunk[:, None]
        s = jnp.where(mask[None, None], s, -jnp.inf)
        p = jax.nn.softmax(s, axis=-1).astype(v.dtype)
        return jnp.einsum('bhqk,bkhd->bqhd', p, v)

    out = lax.map(attend_block, jnp.arange(n_blocks))
    return jnp.moveaxis(out, 0, 1).reshape(B, S, MLA_WIDTH)


def _multiscale_pool(u, pool_w, pool_b, pool_scale):
    B, S, _ = u.shape
    ug = u.reshape(B, S, N_POOL_GROUPS, POOL_GROUP_DIM)
    cs = jnp.cumsum(ug.astype(jnp.float32), axis=1)
    t = jnp.arange(1, S + 1, dtype=jnp.float32)
    means = []
    for gi, w in enumerate(POOL_WINDOWS):
        prev = jnp.pad(cs[:, :-w, gi], ((0, 0), (w, 0), (0, 0)))
        cnt = jnp.minimum(t, float(w))[None, :, None]
        means.append((cs[:, :, gi] - prev) / cnt)
    pooled = jnp.stack(means, axis=2).astype(u.dtype) - ug
    y = jnp.einsum('bsgc,gcd->bsgd', pooled, pool_w) + pool_b
    return y.reshape(B, S, POOL_WIDTH) * pool_scale


def setup_inputs(seed: int = 0) -> dict:
    key = jax.random.key(seed)
    ks = jax.random.split(key, 24)
    f32 = jnp.float32

    def nrm(k, shape, scale):
        return jax.random.normal(k, shape, f32) * scale

    x = jax.random.normal(ks[0], (BATCH, SEQ, D_MODEL), f32)
    c = jax.random.normal(ks[1], (BATCH, D_MODEL), f32)
    offsets = jax.random.randint(ks[2], (BATCH, 1), 0, 1024, dtype=jnp.int32)
    positions = offsets + jnp.arange(SEQ, dtype=jnp.int32)[None, :]
    return {
        'x': x,
        'c': c,
        'positions': positions,
        'ln_in_g': 1.0 + nrm(ks[3], (D_MODEL,), 0.02),
        'ln_in_b': nrm(ks[4], (D_MODEL,), 0.02),
        'w_ada': nrm(ks[5], (DEPTH, D_MODEL, N_SUBLAYERS * 3 * D_MODEL), 0.5 * D_MODEL ** -0.5),
        'b_ada': nrm(ks[6], (DEPTH, N_SUBLAYERS * 3 * D_MODEL), 0.02),
        'ffn1_w_in': nrm(ks[7], (DEPTH, D_MODEL, 2 * D_FF), D_MODEL ** -0.5),
        'ffn1_w_out': nrm(ks[8], (DEPTH, D_FF, D_MODEL), BETA * D_FF ** -0.5),
        'w_in': nrm(ks[9], (DEPTH, D_MODEL, IN_COLS), D_MODEL ** -0.5),
        'q_norm_g': 1.0 + nrm(ks[10], (DEPTH, Q_LORA_RANK), 0.02),
        'kv_norm_g': 1.0 + nrm(ks[11], (DEPTH, KV_LORA_RANK), 0.02),
        'w_uq': nrm(ks[12], (DEPTH, Q_LORA_RANK, N_HEADS * (QK_NOPE_DIM + QK_ROPE_DIM)), Q_LORA_RANK ** -0.5),
        'w_ukv': nrm(ks[13], (DEPTH, KV_LORA_RANK, N_HEADS * (QK_NOPE_DIM + V_HEAD_DIM)), KV_LORA_RANK ** -0.5),
        'pool_w': nrm(ks[14], (DEPTH, N_POOL_GROUPS, POOL_GROUP_DIM, POOL_GROUP_DIM), POOL_GROUP_DIM ** -0.5),
        'pool_b': nrm(ks[15], (DEPTH, N_POOL_GROUPS, POOL_GROUP_DIM), 0.02),
        'pool_scale': 1.0 + nrm(ks[16], (DEPTH, POOL_WIDTH), 0.02),
        'w_out': nrm(ks[17], (DEPTH, MIX_WIDTH, D_MODEL), BETA * MIX_WIDTH ** -0.5),
        'ffn2_w_in': nrm(ks[18], (DEPTH, D_MODEL, 2 * D_FF), D_MODEL ** -0.5),
        'ffn2_w_out': nrm(ks[19], (DEPTH, D_FF, D_MODEL), BETA * D_FF ** -0.5),
        'post_ln_g': 1.0 + nrm(ks[20], (DEPTH, N_SUBLAYERS, D_MODEL), 0.02),
        'post_ln_b': nrm(ks[21], (DEPTH, N_SUBLAYERS, D_MODEL), 0.02),
    }


def reference(x, c, positions, ln_in_g, ln_in_b, w_ada, b_ada, ffn1_w_in, ffn1_w_out,
              w_in, q_norm_g, kv_norm_g, w_uq, w_ukv, pool_w, pool_b, pool_scale, w_out,
              ffn2_w_in, ffn2_w_out, post_ln_g, post_ln_b):
    B, S, D = x.shape
    x = _layernorm(x, ln_in_g, ln_in_b)
    c_act = jax.nn.silu(c)
    split_at = [Q_LORA_RANK, Q_LORA_RANK + KV_LORA_RANK, Q_LORA_RANK + KV_LORA_RANK + QK_ROPE_DIM]
    for l in range(DEPTH):
        mod = (c_act @ w_ada[l] + b_ada[l]).reshape(B, N_SUBLAYERS, 3, D)
        shift, scale, gate = mod[:, :, 0], mod[:, :, 1], mod[:, :, 2]

        h = x * (1.0 + scale[:, 0, None, :]) + shift[:, 0, None, :]
        y = _swiglu(h, ffn1_w_in[l], ffn1_w_out[l])
        x = _layernorm(ALPHA * x + 0.5 * gate[:, 0, None, :] * y, post_ln_g[l, 0], post_ln_b[l, 0])

        h = x * (1.0 + scale[:, 1, None, :]) + shift[:, 1, None, :]
        proj = h @ w_in[l]
        cq, ckv, k_rope_raw, u = jnp.split(proj, split_at, axis=-1)
        attn = _mla(cq, ckv, k_rope_raw, positions, q_norm_g[l], kv_norm_g[l], w_uq[l], w_ukv[l])
        pooled = _multiscale_pool(u, pool_w[l], pool_b[l], pool_scale[l])
        y = jnp.concatenate([attn, pooled], axis=-1) @ w_out[l]
        x = _layernorm(ALPHA * x + gate[:, 1, None, :] * y, post_ln_g[l, 1], post_ln_b[l, 1])

        h = x * (1.0 + scale[:, 2, None, :]) + shift[:, 2, None, :]
        y = _swiglu(h, ffn2_w_in[l], ffn2_w_out[l])
        x = _layernorm(ALPHA * x + 0.5 * gate[:, 2, None, :] * y, post_ln_g[l, 2], post_ln_b[l, 2])
    return x
```

```python
import functools

import jax
import jax.numpy as jnp
from jax import lax
from jax.experimental import pallas as pl
from jax.experimental.pallas import tpu as pltpu

f32 = jnp.float32
bf16 = jnp.bfloat16

D_MODEL = 1024
D_FF = 2816
N_HEADS = 8
QK_NOPE = 64
QK_ROPE = 32
V_DIM = 64
Q_LORA = 512
KV_LORA = 256
HEAD_PAD = 128
POOL_WINDOWS = (2, 4, 8, 16)
POOL_GROUP = 128
POOL_WIDTH = 512
POOL_HALO = 16
CHUNK = 64
ROPE_THETA = 10000.0
LN_EPS = 1e-5
RMS_EPS = 1e-6
ALPHA = 2.0 ** 0.25
SM_SCALE = (QK_NOPE + QK_ROPE) ** -0.5
IN_PAD = Q_LORA + KV_LORA + HEAD_PAD + POOL_WIDTH

TM_FFN = 512
TM_MIX = 512
TQ = 256
FF_CHUNKS = (1024, 1024, 768)
VMEM_LIMIT = 56 * 1024 * 1024


def _layernorm(x, g, b):
    mu = jnp.mean(x, axis=-1, keepdims=True)
    xc = x - mu
    var = jnp.mean(xc * xc, axis=-1, keepdims=True)
    return xc * lax.rsqrt(var + LN_EPS) * g + b


def _rmsnorm(x, g):
    ms = jnp.mean(x * x, axis=-1, keepdims=True)
    return x * lax.rsqrt(ms + RMS_EPS) * g


def _swiglu(h, w_in_ref, w_out_ref):
    y = None
    off = 0
    for n in FF_CHUNKS:
        g = jnp.dot(h, w_in_ref[:, off:off + n], preferred_element_type=f32)
        u = jnp.dot(h, w_in_ref[:, D_FF + off:D_FF + off + n], preferred_element_type=f32)
        a = (g / (1.0 + jnp.exp(-g)) * u).astype(bf16)
        part = jnp.dot(a, w_out_ref[off:off + n, :], preferred_element_type=f32)
        y = part if y is None else y + part
        off += n
    return y


def _ada_kernel(c_ref, w_ref, b_ref, o_ref):
    c = c_ref[...]
    c_act = c / (1.0 + jnp.exp(-c))
    o_ref[...] = jnp.dot(c_act, w_ref[...], preferred_element_type=f32,
                         precision=lax.Precision.HIGHEST) + b_ref[...]


def _ada(c, w_ada, b_ada):
    B = c.shape[0]
    n_out = w_ada.shape[1]
    bn = D_MODEL
    return pl.pallas_call(
        _ada_kernel,
        out_shape=jax.ShapeDtypeStruct((B, n_out), f32),
        grid=(n_out // bn,),
        in_specs=[pl.BlockSpec((B, D_MODEL), lambda j: (0, 0)),
                  pl.BlockSpec((D_MODEL, bn), lambda j: (0, j)),
                  pl.BlockSpec((1, bn), lambda j: (0, j))],
        out_specs=pl.BlockSpec((B, bn), lambda j: (0, j)),
        name="ada",
    )(c, w_ada, b_ada.reshape(1, n_out))


def _rope_kernel(pos_ref, f_ref, cos_ref, sin_ref):
    ang = pos_ref[...].astype(f32) * f_ref[...]
    cos_ref[...] = jnp.cos(ang)
    sin_ref[...] = jnp.sin(ang)


def _rope_tables(pos_rep, freq_row):
    rows = pos_rep.shape[0]
    br = 512
    spec = pl.BlockSpec((br, 128), lambda i: (i, 0))
    return pl.pallas_call(
        _rope_kernel,
        out_shape=(jax.ShapeDtypeStruct((rows, 128), f32),) * 2,
        grid=(rows // br,),
        in_specs=[spec, pl.BlockSpec((1, 128), lambda i: (0, 0))],
        out_specs=(spec, spec),
        name="rope_tables",
    )(pos_rep, freq_row)


def _ffn1_kernel(x_ref, mod_ref, lng_ref, lnb_ref, w_in_ref, w_out_ref, pg_ref, pb_ref, o_ref):
    x = _layernorm(x_ref[0], lng_ref[...], lnb_ref[...])
    shift, scale, gate = mod_ref[0, 0:1, :], mod_ref[0, 1:2, :], mod_ref[0, 2:3, :]
    h = (x * (1.0 + scale) + shift).astype(bf16)
    y = _swiglu(h, w_in_ref, w_out_ref)
    o_ref[0] = _layernorm(ALPHA * x + 0.5 * gate * y, pg_ref[...], pb_ref[...])


def _ffn2_kernel(x_ref, attn_ref, yp_ref, mod_ref, woa_ref, wop_ref, pg1_ref, pb1_ref,
                 w_in_ref, w_out_ref, pg2_ref, pb2_ref, o_ref):
    gate1 = mod_ref[0, 5:6, :]
    y = (jnp.dot(attn_ref[0], woa_ref[...], preferred_element_type=f32)
         + jnp.dot(yp_ref[0], wop_ref[...], preferred_element_type=f32))
    x = _layernorm(ALPHA * x_ref[0] + gate1 * y, pg1_ref[...], pb1_ref[...])
    shift, scale, gate = mod_ref[0, 6:7, :], mod_ref[0, 7:8, :], mod_ref[0, 8:9, :]
    h = (x * (1.0 + scale) + shift).astype(bf16)
    y = _swiglu(h, w_in_ref, w_out_ref)
    o_ref[0] = _layernorm(ALPHA * x + 0.5 * gate * y, pg2_ref[...], pb2_ref[...])


def _resident(shape):
    return pl.BlockSpec(shape, lambda b, i: (0,) * len(shape), pipeline_mode=pl.Buffered(1))


def _row_spec(tm, width):
    return pl.BlockSpec((1, tm, width), lambda b, i: (b, i, 0))


def _mod_spec():
    return pl.BlockSpec((1, 9, D_MODEL), lambda b, i: (b, 0, 0))


def _ffn1(x, mod, lng, lnb, w_in, w_out, pg, pb):
    B, S, D = x.shape
    vec = _resident((1, D))
    return pl.pallas_call(
        _ffn1_kernel,
        out_shape=jax.ShapeDtypeStruct((B, S, D), f32),
        grid=(B, S // TM_FFN),
        in_specs=[_row_spec(TM_FFN, D), _mod_spec(), vec, vec,
                  _resident(w_in.shape), _resident(w_out.shape), vec, vec],
        out_specs=_row_spec(TM_FFN, D),
        compiler_params=pltpu.CompilerParams(
            dimension_semantics=("parallel", "parallel"), vmem_limit_bytes=VMEM_LIMIT),
        name="ffn1",
    )(x, mod, lng, lnb, w_in, w_out, pg, pb)


def _ffn2(x1, attn, yp, mod, woa, wop, pg1, pb1, w_in, w_out, pg2, pb2):
    B, S, D = x1.shape
    vec = _resident((1, D))
    return pl.pallas_call(
        _ffn2_kernel,
        out_shape=jax.ShapeDtypeStruct((B, S, D), f32),
        grid=(B, S // TM_FFN),
        in_specs=[_row_spec(TM_FFN, D), _row_spec(TM_FFN, attn.shape[-1]),
                  _row_spec(TM_FFN, yp.shape[-1]), _mod_spec(),
                  _resident(woa.shape), _resident(wop.shape), vec, vec,
                  _resident(w_in.shape), _resident(w_out.shape), vec, vec],
        out_specs=_row_spec(TM_FFN, D),
        compiler_params=pltpu.CompilerParams(
            dimension_semantics=("parallel", "parallel"), vmem_limit_bytes=VMEM_LIMIT),
        name="ffn2",
    )(x1, attn, yp, mod, woa, wop, pg1, pb1, w_in, w_out, pg2, pb2)


def _apply_rope(t, cos_t, sin_t):
    return t * cos_t + pltpu.roll(t, HEAD_PAD - QK_ROPE, 1) * sin_t


def _mix_kernel(x_ref, mod_ref, cos_ref, sin_ref, w_in_ref, qg_ref, kvg_ref, wuq_ref, wuk_ref,
                wuv_ref, wpool_ref, pbias_ref, pscale_ref,
                q_ref, k_ref, v_ref, yp_ref, tail_ref):
    st = pl.program_id(1)
    tm = x_ref.shape[1]
    shift, scale = mod_ref[0, 3:4, :], mod_ref[0, 4:5, :]
    h = (x_ref[0] * (1.0 + scale) + shift).astype(bf16)
    proj = jnp.dot(h, w_in_ref[...], preferred_element_type=f32)
    cq = proj[:, :Q_LORA]
    ckv = proj[:, Q_LORA:Q_LORA + KV_LORA]
    krb = proj[:, Q_LORA + KV_LORA:Q_LORA + KV_LORA + HEAD_PAD]
    u = proj[:, Q_LORA + KV_LORA + HEAD_PAD:]

    cos_t, sin_t = cos_ref[0], sin_ref[0]

    cqn = _rmsnorm(cq, qg_ref[...]).astype(bf16)
    q_all = jnp.dot(cqn, wuq_ref[...], preferred_element_type=f32)
    ckvn = _rmsnorm(ckv, kvg_ref[...]).astype(bf16)
    k_all = jnp.dot(ckvn, wuk_ref[...], preferred_element_type=f32)
    v_ref[0] = jnp.dot(ckvn, wuv_ref[...], preferred_element_type=f32).astype(bf16)
    k_rope = _apply_rope(krb, cos_t, sin_t)
    for hh in range(N_HEADS):
        sl = slice(hh * HEAD_PAD, (hh + 1) * HEAD_PAD)
        q_ref[0, :, sl] = (_apply_rope(q_all[:, sl], cos_t, sin_t) * SM_SCALE).astype(bf16)
        k_ref[0, :, sl] = (k_all[:, sl] + k_rope).astype(bf16)

    @pl.when(st == 0)
    def _():
        tail_ref[...] = jnp.zeros_like(tail_ref)

    ext = jnp.concatenate([tail_ref[...], u], axis=0)
    tail_ref[...] = u[tm - POOL_HALO:, :]
    t_idx = st * tm + lax.broadcasted_iota(jnp.int32, (tm, 1), 0)
    groups = []
    for gi, w in enumerate(POOL_WINDOWS):
        e = ext[:, gi * POOL_GROUP:(gi + 1) * POOL_GROUP]
        sh = 1
        while sh < w:
            e = e + pltpu.roll(e, sh, 0)
            sh *= 2
        cnt = jnp.minimum(t_idx + 1, w).astype(f32)
        groups.append(e[POOL_HALO:, :] / cnt - u[:, gi * POOL_GROUP:(gi + 1) * POOL_GROUP])
    pooled = jnp.concatenate(groups, axis=1).astype(bf16)
    yp = jnp.dot(pooled, wpool_ref[...], preferred_element_type=f32) + pbias_ref[...]
    yp_ref[0] = (yp * pscale_ref[...]).astype(bf16)


def _mix(x1, mod, cos_t, sin_t, w_in, qg, kvg, wuq, wuk, wuv, wpool, pbias, pscale):
    B, S, D = x1.shape
    tm = TM_MIX
    hw = N_HEADS * HEAD_PAD

    def const(shape):
        return pl.BlockSpec(shape, lambda b, i: (0,) * len(shape))

    return pl.pallas_call(
        _mix_kernel,
        out_shape=(jax.ShapeDtypeStruct((B, S, hw), bf16),
                   jax.ShapeDtypeStruct((B, S, hw), bf16),
                   jax.ShapeDtypeStruct((B, S, hw), bf16),
                   jax.ShapeDtypeStruct((B, S, POOL_WIDTH), bf16)),
        grid=(B, S // tm),
        in_specs=[_row_spec(tm, D), _mod_spec(), _row_spec(tm, HEAD_PAD), _row_spec(tm, HEAD_PAD),
                  const(w_in.shape), const(qg.shape), const(kvg.shape), const(wuq.shape),
                  const(wuk.shape), const(wuv.shape), const(wpool.shape), const(pbias.shape),
                  const(pscale.shape)],
        out_specs=(_row_spec(tm, hw), _row_spec(tm, hw), _row_spec(tm, hw),
                   _row_spec(tm, POOL_WIDTH)),
        scratch_shapes=[pltpu.VMEM((POOL_HALO, POOL_WIDTH), f32)],
        compiler_params=pltpu.CompilerParams(
            dimension_semantics=("parallel", "arbitrary"), vmem_limit_bytes=VMEM_LIMIT),
        name="mix_proj",
    )(x1, mod, cos_t, sin_t, w_in, qg, kvg, wuq, wuk, wuv, wpool, pbias, pscale)


def _attn_kernel(q_ref, k_ref, v_ref, o_ref, m_ref, l_ref, acc_ref):
    i = pl.program_id(1)
    tq = q_ref.shape[1]
    m_ref[...] = jnp.full_like(m_ref, -jnp.inf)
    l_ref[...] = jnp.zeros_like(l_ref)
    acc_ref[...] = jnp.zeros_like(acc_ref)

    row_chunk = lax.broadcasted_iota(jnp.int32, (tq, tq), 0) // CHUNK
    col_chunk = lax.broadcasted_iota(jnp.int32, (tq, tq), 1) // CHUNK
    diag_mask = col_chunk <= row_chunk

    def step(j, masked):
        koff = pl.multiple_of(j * tq, tq)
        for hh in range(N_HEADS):
            sl = slice(hh * HEAD_PAD, (hh + 1) * HEAD_PAD)
            q = q_ref[0, :, sl]
            k = k_ref[0, pl.ds(koff, tq), sl]
            v = v_ref[0, pl.ds(koff, tq), sl]
            s = lax.dot_general(q, k, (((1,), (1,)), ((), ())), preferred_element_type=f32)
            if masked:
                s = jnp.where(diag_mask, s, -jnp.inf)
            m_prev = m_ref[hh]
            m_new = jnp.maximum(m_prev, jnp.max(s, axis=-1, keepdims=True))
            alpha = jnp.exp(m_prev - m_new)
            p = jnp.exp(s - m_new)
            l_ref[hh] = alpha * l_ref[hh] + jnp.sum(p, axis=-1, keepdims=True)
            acc_ref[hh] = alpha * acc_ref[hh] + jnp.dot(p.astype(bf16), v,
                                                        preferred_element_type=f32)
            m_ref[hh] = m_new

    def body(j, carry):
        step(j, False)
        return carry

    lax.fori_loop(0, i, body, 0)
    step(i, True)
    for hh in range(N_HEADS):
        sl = slice(hh * HEAD_PAD, (hh + 1) * HEAD_PAD)
        o_ref[0, :, sl] = (acc_ref[hh] / l_ref[hh]).astype(bf16)


def _attention(q, k, v):
    B, S, hw = q.shape
    seq = pl.BlockSpec((1, S, hw), lambda b, i: (b, 0, 0))
    return pl.pallas_call(
        _attn_kernel,
        out_shape=jax.ShapeDtypeStruct((B, S, hw), bf16),
        grid=(B, S // TQ),
        in_specs=[_row_spec(TQ, hw), seq, seq],
        out_specs=_row_spec(TQ, hw),
        scratch_shapes=[pltpu.VMEM((N_HEADS, TQ, 1), f32),
                        pltpu.VMEM((N_HEADS, TQ, 1), f32),
                        pltpu.VMEM((N_HEADS, TQ, HEAD_PAD), f32)],
        compiler_params=pltpu.CompilerParams(
            dimension_semantics=("parallel", "parallel"), vmem_limit_bytes=VMEM_LIMIT),
        name="attention",
    )(q, k, v)


def _rot_cols(w):
    half = w.shape[-1] // 2
    return jnp.concatenate([-w[..., half:], w[..., :half]], axis=-1)


def _head_slots(parts):
    k_dim = parts[0][0].shape[0]
    cols = []
    for blocks in parts:
        used = sum(b.shape[1] for b in blocks)
        cols.extend(blocks)
        if used < HEAD_PAD:
            cols.append(jnp.zeros((k_dim, HEAD_PAD - used), f32))
    return jnp.concatenate(cols, axis=1)


def kernel(x, c, positions, ln_in_g, ln_in_b, w_ada, b_ada, ffn1_w_in, ffn1_w_out, w_in, q_norm_g, kv_norm_g, w_uq, w_ukv, pool_w, pool_b, pool_scale, w_out, ffn2_w_in, ffn2_w_out, post_ln_g, post_ln_b):
    B, S, D = x.shape
    l = 0
    row = lambda a: a.reshape(1, -1)

    qh = QK_NOPE + QK_ROPE
    wuq = w_uq[l]
    wuq_pad = _head_slots([[wuq[:, h * qh:h * qh + QK_NOPE],
                            wuq[:, h * qh + QK_NOPE:(h + 1) * qh],
                            _rot_cols(wuq[:, h * qh + QK_NOPE:(h + 1) * qh])]
                           for h in range(N_HEADS)]).astype(bf16)
    kvh = QK_NOPE + V_DIM
    wukv = w_ukv[l]
    wuk_pad = _head_slots([[wukv[:, h * kvh:h * kvh + QK_NOPE]] for h in range(N_HEADS)]).astype(bf16)
    wuv_pad = _head_slots([[wukv[:, h * kvh + QK_NOPE:(h + 1) * kvh]] for h in range(N_HEADS)]).astype(bf16)
    w_kr = w_in[l][:, Q_LORA + KV_LORA:Q_LORA + KV_LORA + QK_ROPE]
    w_in_pad = jnp.concatenate(
        [w_in[l][:, :Q_LORA + KV_LORA], jnp.zeros((D, QK_NOPE), f32), w_kr, _rot_cols(w_kr),
         w_in[l][:, Q_LORA + KV_LORA + QK_ROPE:]], axis=1).astype(bf16)
    wpool = jax.scipy.linalg.block_diag(*[pool_w[l, g] for g in range(len(POOL_WINDOWS))]).astype(bf16)
    wo = w_out[l]
    woa_pad = jnp.concatenate(
        [jnp.concatenate([wo[h * V_DIM:(h + 1) * V_DIM], jnp.zeros((HEAD_PAD - V_DIM, D), f32)], axis=0)
         for h in range(N_HEADS)], axis=0).astype(bf16)
    wop = wo[N_HEADS * V_DIM:].astype(bf16)

    inv_freq = ROPE_THETA ** (-jnp.arange(0, QK_ROPE, 2, dtype=f32) / QK_ROPE)
    pos_rep = jnp.repeat(positions.reshape(B * S // 8, 8), QK_ROPE // 2, axis=1)
    cos_c, sin_c = _rope_tables(pos_rep, jnp.tile(inv_freq, 8).reshape(1, 128))
    cos_c = cos_c.reshape(B, S, QK_ROPE // 2)
    sin_c = sin_c.reshape(B, S, QK_ROPE // 2)
    cos_t = jnp.concatenate([jnp.ones((B, S, QK_NOPE), f32), cos_c, cos_c,
                             jnp.zeros((B, S, QK_ROPE), f32)], axis=-1)
    sin_t = jnp.concatenate([jnp.zeros((B, S, QK_NOPE), f32), sin_c, sin_c,
                             jnp.zeros((B, S, QK_ROPE), f32)], axis=-1)

    mod = _ada(c, w_ada[l], b_ada[l]).reshape(B, 9, D)

    x1 = _ffn1(x, mod, row(ln_in_g), row(ln_in_b), ffn1_w_in[l].astype(bf16),
               ffn1_w_out[l].astype(bf16), row(post_ln_g[l, 0]), row(post_ln_b[l, 0]))
    q, k, v, yp = _mix(x1, mod, cos_t, sin_t, w_in_pad, row(q_norm_g[l]), row(kv_norm_g[l]),
                       wuq_pad, wuk_pad, wuv_pad, wpool, row(pool_b[l]), row(pool_scale[l]))
    attn = _attention(q, k, v)
    return _ffn2(x1, attn, yp, mod, woa_pad, wop, row(post_ln_g[l, 1]), row(post_ln_b[l, 1]),
                 ffn2_w_in[l].astype(bf16), ffn2_w_out[l].astype(bf16),
                 row(post_ln_g[l, 2]), row(post_ln_b[l, 2]))
```

```python
import jax
import jax.numpy as jnp
from jax import lax
from jax.experimental import pallas as pl
from jax.experimental.pallas import tpu as pltpu

f32 = jnp.float32
bf16 = jnp.bfloat16

D_MODEL = 1024
D_FF = 2816
N_HEADS = 8
QK_NOPE = 64
QK_ROPE = 32
V_DIM = 64
Q_LORA = 512
KV_LORA = 256
HEAD_PAD = 128
POOL_WINDOWS = (2, 4, 8, 16)
POOL_GROUP = 128
POOL_WIDTH = 512
POOL_HALO = 16
CHUNK = 64
ROPE_THETA = 10000.0
LN_EPS = 1e-5
RMS_EPS = 1e-6
ALPHA = 2.0 ** 0.25
SM_SCALE = (QK_NOPE + QK_ROPE) ** -0.5
IN_PAD = Q_LORA + KV_LORA + HEAD_PAD + POOL_WIDTH

TM_FFN = 512
TM_MIX = 512
TQ = 256
TK = 256
FF_CHUNKS = (1024, 1024, 768)
VMEM_LIMIT = 56 * 1024 * 1024

_NT = (((1,), (1,)), ((), ()))


def _layernorm(x, g, b):
    mu = jnp.mean(x, axis=-1, keepdims=True)
    xc = x - mu
    var = jnp.mean(xc * xc, axis=-1, keepdims=True)
    return xc * lax.rsqrt(var + LN_EPS) * g + b


def _rmsnorm(x, g):
    ms = jnp.mean(x * x, axis=-1, keepdims=True)
    return x * lax.rsqrt(ms + RMS_EPS) * g


def _swiglu(h, w_in_ref, w_out_ref):
    y = None
    off = 0
    for n in FF_CHUNKS:
        g = jnp.dot(h, w_in_ref[:, off:off + n], preferred_element_type=f32)
        u = jnp.dot(h, w_in_ref[:, D_FF + off:D_FF + off + n], preferred_element_type=f32)
        a = (g / (1.0 + jnp.exp(-g)) * u).astype(bf16)
        part = jnp.dot(a, w_out_ref[off:off + n, :], preferred_element_type=f32)
        y = part if y is None else y + part
        off += n
    return y


def _ada_kernel(c_ref, w_ref, b_ref, o_ref):
    c = c_ref[...]
    c_act = c / (1.0 + jnp.exp(-c))
    o_ref[...] = jnp.dot(c_act, w_ref[...], preferred_element_type=f32,
                         precision=lax.Precision.HIGHEST) + b_ref[...]


def _ada(c, w_ada, b_ada):
    B = c.shape[0]
    n_out = w_ada.shape[1]
    bn = D_MODEL
    return pl.pallas_call(
        _ada_kernel,
        out_shape=jax.ShapeDtypeStruct((B, n_out), f32),
        grid=(n_out // bn,),
        in_specs=[pl.BlockSpec((B, D_MODEL), lambda j: (0, 0)),
                  pl.BlockSpec((D_MODEL, bn), lambda j: (0, j)),
                  pl.BlockSpec((1, bn), lambda j: (0, j))],
        out_specs=pl.BlockSpec((B, bn), lambda j: (0, j)),
        name="ada",
    )(c, w_ada, b_ada.reshape(1, n_out))


def _rope_kernel(pos_ref, f_ref, cos_ref, sin_ref):
    ang = pos_ref[...].astype(f32) * f_ref[...]
    cos_ref[...] = jnp.cos(ang)
    sin_ref[...] = jnp.sin(ang)


def _rope_tables(pos_rep, freq_row):
    rows = pos_rep.shape[0]
    br = 512
    spec = pl.BlockSpec((br, 128), lambda i: (i, 0))
    return pl.pallas_call(
        _rope_kernel,
        out_shape=(jax.ShapeDtypeStruct((rows, 128), f32),) * 2,
        grid=(rows // br,),
        in_specs=[spec, pl.BlockSpec((1, 128), lambda i: (0, 0))],
        out_specs=(spec, spec),
        name="rope_tables",
    )(pos_rep, freq_row)


def _ffn1_kernel(x_ref, mod_ref, lng_ref, lnb_ref, w_in_ref, w_out_ref, pg_ref, pb_ref, o_ref):
    x = _layernorm(x_ref[0], lng_ref[...], lnb_ref[...])
    shift, scale, gate = mod_ref[0, 0:1, :], mod_ref[0, 1:2, :], mod_ref[0, 2:3, :]
    h = (x * (1.0 + scale) + shift).astype(bf16)
    y = _swiglu(h, w_in_ref, w_out_ref)
    o_ref[0] = _layernorm(ALPHA * x + 0.5 * gate * y, pg_ref[...], pb_ref[...])


def _ffn2_kernel(x_ref, attn_ref, yp_ref, mod_ref, woa_ref, wop_ref, pg1_ref, pb1_ref,
                 w_in_ref, w_out_ref, pg2_ref, pb2_ref, o_ref):
    gate1 = mod_ref[0, 5:6, :]
    y = (jnp.dot(attn_ref[0], woa_ref[...], preferred_element_type=f32)
         + jnp.dot(yp_ref[0], wop_ref[...], preferred_element_type=f32))
    x = _layernorm(ALPHA * x_ref[0] + gate1 * y, pg1_ref[...], pb1_ref[...])
    shift, scale, gate = mod_ref[0, 6:7, :], mod_ref[0, 7:8, :], mod_ref[0, 8:9, :]
    h = (x * (1.0 + scale) + shift).astype(bf16)
    y = _swiglu(h, w_in_ref, w_out_ref)
    o_ref[0] = _layernorm(ALPHA * x + 0.5 * gate * y, pg2_ref[...], pb2_ref[...])


def _resident(shape):
    return pl.BlockSpec(shape, lambda b, i: (0,) * len(shape), pipeline_mode=pl.Buffered(1))


def _row_spec(tm, width):
    return pl.BlockSpec((1, tm, width), lambda b, i: (b, i, 0))


def _mod_spec():
    return pl.BlockSpec((1, 9, D_MODEL), lambda b, i: (b, 0, 0))


def _ffn1(x, mod, lng, lnb, w_in, w_out, pg, pb):
    B, S, D = x.shape
    vec = _resident((1, D))
    return pl.pallas_call(
        _ffn1_kernel,
        out_shape=jax.ShapeDtypeStruct((B, S, D), f32),
        grid=(B, S // TM_FFN),
        in_specs=[_row_spec(TM_FFN, D), _mod_spec(), vec, vec,
                  _resident(w_in.shape), _resident(w_out.shape), vec, vec],
        out_specs=_row_spec(TM_FFN, D),
        compiler_params=pltpu.CompilerParams(
            dimension_semantics=("parallel", "parallel"), vmem_limit_bytes=VMEM_LIMIT),
        name="ffn1",
    )(x, mod, lng, lnb, w_in, w_out, pg, pb)


def _ffn2(x1, attn, yp, mod, woa, wop, pg1, pb1, w_in, w_out, pg2, pb2):
    B, S, D = x1.shape
    vec = _resident((1, D))
    return pl.pallas_call(
        _ffn2_kernel,
        out_shape=jax.ShapeDtypeStruct((B, S, D), f32),
        grid=(B, S // TM_FFN),
        in_specs=[_row_spec(TM_FFN, D), _row_spec(TM_FFN, attn.shape[-1]),
                  _row_spec(TM_FFN, yp.shape[-1]), _mod_spec(),
                  _resident(woa.shape), _resident(wop.shape), vec, vec,
                  _resident(w_in.shape), _resident(w_out.shape), vec, vec],
        out_specs=_row_spec(TM_FFN, D),
        compiler_params=pltpu.CompilerParams(
            dimension_semantics=("parallel", "parallel"), vmem_limit_bytes=VMEM_LIMIT),
        name="ffn2",
    )(x1, attn, yp, mod, woa, wop, pg1, pb1, w_in, w_out, pg2, pb2)


def _apply_rope(t, cos_t, sin_t):
    return t * cos_t + pltpu.roll(t, HEAD_PAD - QK_ROPE, 1) * sin_t


def _mix_kernel(x_ref, mod_ref, cos_ref, sin_ref, w_in_ref, qg_ref, kvg_ref, wuq_ref, wuk_ref,
                wuvt_ref, wpool_ref, pbias_ref, pscale_ref,
                q_ref, k_ref, vt_ref, yp_ref, tail_ref):
    st = pl.program_id(1)
    tm = x_ref.shape[1]
    shift, scale = mod_ref[0, 3:4, :], mod_ref[0, 4:5, :]
    h = (x_ref[0] * (1.0 + scale) + shift).astype(bf16)
    proj = jnp.dot(h, w_in_ref[...], preferred_element_type=f32)
    cq = proj[:, :Q_LORA]
    ckv = proj[:, Q_LORA:Q_LORA + KV_LORA]
    krb = proj[:, Q_LORA + KV_LORA:Q_LORA + KV_LORA + HEAD_PAD]
    u = proj[:, Q_LORA + KV_LORA + HEAD_PAD:]

    cos_t, sin_t = cos_ref[0], sin_ref[0]

    cqn = _rmsnorm(cq, qg_ref[...]).astype(bf16)
    q_all = jnp.dot(cqn, wuq_ref[...], preferred_element_type=f32)
    ckvn = _rmsnorm(ckv, kvg_ref[...]).astype(bf16)
    k_all = jnp.dot(ckvn, wuk_ref[...], preferred_element_type=f32)
    v_t = lax.dot_general(wuvt_ref[...], ckvn, _NT, preferred_element_type=f32).astype(bf16)
    for t in range(tm // TK):
        vt_ref[0, t] = v_t[:, t * TK:(t + 1) * TK]
    k_rope = _apply_rope(krb, cos_t, sin_t)
    for hh in range(N_HEADS):
        sl = slice(hh * HEAD_PAD, (hh + 1) * HEAD_PAD)
        q_ref[0, :, sl] = (_apply_rope(q_all[:, sl], cos_t, sin_t) * SM_SCALE).astype(bf16)
        k_ref[0, :, sl] = (k_all[:, sl] + k_rope).astype(bf16)

    @pl.when(st == 0)
    def _():
        tail_ref[...] = jnp.zeros_like(tail_ref)

    ext = jnp.concatenate([tail_ref[...], u], axis=0)
    tail_ref[...] = u[tm - POOL_HALO:, :]
    t_idx = st * tm + lax.broadcasted_iota(jnp.int32, (tm, 1), 0)
    groups = []
    for gi, w in enumerate(POOL_WINDOWS):
        e = ext[:, gi * POOL_GROUP:(gi + 1) * POOL_GROUP]
        sh = 1
        while sh < w:
            e = e + pltpu.roll(e, sh, 0)
            sh *= 2
        cnt = jnp.minimum(t_idx + 1, w).astype(f32)
        groups.append(e[POOL_HALO:, :] / cnt - u[:, gi * POOL_GROUP:(gi + 1) * POOL_GROUP])
    pooled = jnp.concatenate(groups, axis=1).astype(bf16)
    yp = jnp.dot(pooled, wpool_ref[...], preferred_element_type=f32) + pbias_ref[...]
    yp_ref[0] = (yp * pscale_ref[...]).astype(bf16)


def _mix(x1, mod, cos_t, sin_t, w_in, qg, kvg, wuq, wuk, wuvt, wpool, pbias, pscale):
    B, S, D = x1.shape
    tm = TM_MIX
    hw = N_HEADS * HEAD_PAD
    vw = N_HEADS * V_DIM

    def const(shape):
        return pl.BlockSpec(shape, lambda b, i: (0,) * len(shape))

    return pl.pallas_call(
        _mix_kernel,
        out_shape=(jax.ShapeDtypeStruct((B, S, hw), bf16),
                   jax.ShapeDtypeStruct((B, S, hw), bf16),
                   jax.ShapeDtypeStruct((B, S // TK, vw, TK), bf16),
                   jax.ShapeDtypeStruct((B, S, POOL_WIDTH), bf16)),
        grid=(B, S // tm),
        in_specs=[_row_spec(tm, D), _mod_spec(), _row_spec(tm, HEAD_PAD), _row_spec(tm, HEAD_PAD),
                  const(w_in.shape), const(qg.shape), const(kvg.shape), const(wuq.shape),
                  const(wuk.shape), const(wuvt.shape), const(wpool.shape), const(pbias.shape),
                  const(pscale.shape)],
        out_specs=(_row_spec(tm, hw), _row_spec(tm, hw),
                   pl.BlockSpec((1, tm // TK, vw, TK), lambda b, i: (b, i, 0, 0)),
                   _row_spec(tm, POOL_WIDTH)),
        scratch_shapes=[pltpu.VMEM((POOL_HALO, POOL_WIDTH), f32)],
        compiler_params=pltpu.CompilerParams(
            dimension_semantics=("parallel", "arbitrary"), vmem_limit_bytes=VMEM_LIMIT),
        name="mix_proj",
    )(x1, mod, cos_t, sin_t, w_in, qg, kvg, wuq, wuk, wuvt, wpool, pbias, pscale)


def _attn_kernel(q_ref, k_ref, vt_ref, o_ref, m_ref, l_ref, acc_ref):
    i = pl.program_id(1)
    m_ref[...] = jnp.full_like(m_ref, -jnp.inf)
    l_ref[...] = jnp.zeros_like(l_ref)
    acc_ref[...] = jnp.zeros_like(acc_ref)

    key_chunk = lax.broadcasted_iota(jnp.int32, (TK, TQ), 0) // CHUNK
    qry_chunk = lax.broadcasted_iota(jnp.int32, (TK, TQ), 1) // CHUNK
    diag_mask = key_chunk <= qry_chunk

    def step(j, masked):
        koff = pl.multiple_of(j * TK, TK)

        def scores(hh):
            sl = slice(hh * HEAD_PAD, (hh + 1) * HEAD_PAD)
            return lax.dot_general(k_ref[0, pl.ds(koff, TK), sl], q_ref[0, :, sl], _NT,
                                   preferred_element_type=f32)

        s_all = [scores(hh) for hh in range(N_HEADS)]
        for hh in range(N_HEADS):
            s = s_all[hh]
            if masked:
                s = jnp.where(diag_mask, s, -jnp.inf)
            m_prev = m_ref[hh]
            m_new = jnp.maximum(m_prev, jnp.max(s, axis=0, keepdims=True))
            alpha = jnp.exp(m_prev - m_new)
            p = jnp.exp(s - m_new)
            l_ref[hh] = alpha * l_ref[hh] + jnp.sum(p, axis=0, keepdims=True)
            v_t = vt_ref[0, j, hh * V_DIM:(hh + 1) * V_DIM, :]
            acc_ref[hh] = alpha * acc_ref[hh] + jnp.dot(v_t, p.astype(bf16),
                                                        preferred_element_type=f32)
            m_ref[hh] = m_new

    def body(j, carry):
        step(j, False)
        return carry

    lax.fori_loop(0, i, body, 0)
    step(i, True)
    o_t = jnp.concatenate([acc_ref[hh] / l_ref[hh] for hh in range(N_HEADS)], axis=0)
    o_ref[0] = o_t.T.astype(bf16)


def _attention(q, k, v_t):
    B, S, hw = q.shape
    vw = N_HEADS * V_DIM
    return pl.pallas_call(
        _attn_kernel,
        out_shape=jax.ShapeDtypeStruct((B, S, vw), bf16),
        grid=(B, S // TQ),
        in_specs=[_row_spec(TQ, hw),
                  pl.BlockSpec((1, S, hw), lambda b, i: (b, 0, 0)),
                  pl.BlockSpec((1, S // TK, vw, TK), lambda b, i: (b, 0, 0, 0))],
        out_specs=_row_spec(TQ, vw),
        scratch_shapes=[pltpu.VMEM((N_HEADS, 1, TQ), f32),
                        pltpu.VMEM((N_HEADS, 1, TQ), f32),
                        pltpu.VMEM((N_HEADS, V_DIM, TQ), f32)],
        compiler_params=pltpu.CompilerParams(
            dimension_semantics=("parallel", "parallel"), vmem_limit_bytes=VMEM_LIMIT),
        name="attention",
    )(q, k, v_t)


def _rot_cols(w):
    half = w.shape[-1] // 2
    return jnp.concatenate([-w[..., half:], w[..., :half]], axis=-1)


def _head_slots(parts):
    k_dim = parts[0][0].shape[0]
    cols = []
    for blocks in parts:
        used = sum(b.shape[1] for b in blocks)
        cols.extend(blocks)
        if used < HEAD_PAD:
            cols.append(jnp.zeros((k_dim, HEAD_PAD - used), f32))
    return jnp.concatenate(cols, axis=1)


def kernel(x, c, positions, ln_in_g, ln_in_b, w_ada, b_ada, ffn1_w_in, ffn1_w_out, w_in, q_norm_g, kv_norm_g, w_uq, w_ukv, pool_w, pool_b, pool_scale, w_out, ffn2_w_in, ffn2_w_out, post_ln_g, post_ln_b):
    B, S, D = x.shape
    l = 0
    row = lambda a: a.reshape(1, -1)

    qh = QK_NOPE + QK_ROPE
    wuq = w_uq[l]
    wuq_pad = _head_slots([[wuq[:, h * qh:h * qh + QK_NOPE],
                            wuq[:, h * qh + QK_NOPE:(h + 1) * qh],
                            _rot_cols(wuq[:, h * qh + QK_NOPE:(h + 1) * qh])]
                           for h in range(N_HEADS)]).astype(bf16)
    kvh = QK_NOPE + V_DIM
    wukv = w_ukv[l]
    wuk_pad = _head_slots([[wukv[:, h * kvh:h * kvh + QK_NOPE]] for h in range(N_HEADS)]).astype(bf16)
    wuv_t = jnp.concatenate([wukv[:, h * kvh + QK_NOPE:(h + 1) * kvh] for h in range(N_HEADS)],
                            axis=1).T.astype(bf16)
    w_kr = w_in[l][:, Q_LORA + KV_LORA:Q_LORA + KV_LORA + QK_ROPE]
    w_in_pad = jnp.concatenate(
        [w_in[l][:, :Q_LORA + KV_LORA], jnp.zeros((D, QK_NOPE), f32), w_kr, _rot_cols(w_kr),
         w_in[l][:, Q_LORA + KV_LORA + QK_ROPE:]], axis=1).astype(bf16)
    wpool = jax.scipy.linalg.block_diag(*[pool_w[l, g] for g in range(len(POOL_WINDOWS))]).astype(bf16)
    wo = w_out[l]
    woa = wo[:N_HEADS * V_DIM].astype(bf16)
    wop = wo[N_HEADS * V_DIM:].astype(bf16)

    inv_freq = ROPE_THETA ** (-jnp.arange(0, QK_ROPE, 2, dtype=f32) / QK_ROPE)
    pos_rep = jnp.repeat(positions.reshape(B * S // 8, 8), QK_ROPE // 2, axis=1)
    cos_c, sin_c = _rope_tables(pos_rep, jnp.tile(inv_freq, 8).reshape(1, 128))
    cos_c = cos_c.reshape(B, S, QK_ROPE // 2)
    sin_c = sin_c.reshape(B, S, QK_ROPE // 2)
    cos_t = jnp.concatenate([jnp.ones((B, S, QK_NOPE), f32), cos_c, cos_c,
                             jnp.zeros((B, S, QK_ROPE), f32)], axis=-1)
    sin_t = jnp.concatenate([jnp.zeros((B, S, QK_NOPE), f32), sin_c, sin_c,
                             jnp.zeros((B, S, QK_ROPE), f32)], axis=-1)

    mod = _ada(c, w_ada[l], b_ada[l]).reshape(B, 9, D)

    x1 = _ffn1(x, mod, row(ln_in_g), row(ln_in_b), ffn1_w_in[l].astype(bf16),
               ffn1_w_out[l].astype(bf16), row(post_ln_g[l, 0]), row(post_ln_b[l, 0]))
    q, k, v_t, yp = _mix(x1, mod, cos_t, sin_t, w_in_pad, row(q_norm_g[l]), row(kv_norm_g[l]),
                         wuq_pad, wuk_pad, wuv_t, wpool, row(pool_b[l]), row(pool_scale[l]))
    attn = _attention(q, k, v_t)
    return _ffn2(x1, attn, yp, mod, woa, wop, row(post_ln_g[l, 1]), row(post_ln_b[l, 1]),
                 ffn2_w_in[l].astype(bf16), ffn2_w_out[l].astype(bf16),
                 row(post_ln_g[l, 2]), row(post_ln_b[l, 2]))
```

```python
import jax
import jax.numpy as jnp
from jax import lax
from jax.experimental import pallas as pl
from jax.experimental.pallas import tpu as pltpu

f32 = jnp.float32
bf16 = jnp.bfloat16

D_MODEL = 1024
D_FF = 2816
N_HEADS = 8
QK_NOPE = 64
QK_ROPE = 32
V_DIM = 64
Q_LORA = 512
KV_LORA = 256
HEAD_PAD = 128
POOL_WINDOWS = (2, 4, 8, 16)
POOL_GROUP = 128
POOL_WIDTH = 512
POOL_HALO = 16
CHUNK = 64
ROPE_THETA = 10000.0
LN_EPS = 1e-5
RMS_EPS = 1e-6
ALPHA = 2.0 ** 0.25
SM_SCALE = (QK_NOPE + QK_ROPE) ** -0.5
LOG2_E = 1.4426950408889634
V_SLAB = 80
IN_PAD = Q_LORA + KV_LORA + HEAD_PAD + POOL_WIDTH

TM_FFN = 512
TM_MIX = 512
TQ = 256
TK = 256
FF_CHUNKS = (1024, 1024, 768)
VMEM_LIMIT = 56 * 1024 * 1024

_NT = (((1,), (1,)), ((), ()))


def _layernorm(x, g, b):
    mu = jnp.mean(x, axis=-1, keepdims=True)
    xc = x - mu
    var = jnp.mean(xc * xc, axis=-1, keepdims=True)
    return xc * lax.rsqrt(var + LN_EPS) * g + b


def _rmsnorm(x, g):
    ms = jnp.mean(x * x, axis=-1, keepdims=True)
    return x * lax.rsqrt(ms + RMS_EPS) * g


def _swiglu(h, w_in_ref, w_out_ref):
    y = None
    off = 0
    for n in FF_CHUNKS:
        g = jnp.dot(h, w_in_ref[:, off:off + n], preferred_element_type=f32)
        u = jnp.dot(h, w_in_ref[:, D_FF + off:D_FF + off + n], preferred_element_type=f32)
        a = (g / (1.0 + jnp.exp(-g)) * u).astype(bf16)
        part = jnp.dot(a, w_out_ref[off:off + n, :], preferred_element_type=f32)
        y = part if y is None else y + part
        off += n
    return y


def _ada_kernel(c_ref, w_ref, b_ref, o_ref):
    c = c_ref[...]
    c_act = c / (1.0 + jnp.exp(-c))
    o_ref[...] = jnp.dot(c_act, w_ref[...], preferred_element_type=f32,
                         precision=lax.Precision.HIGHEST) + b_ref[...]


def _ada(c, w_ada, b_ada):
    B = c.shape[0]
    n_out = w_ada.shape[1]
    bn = D_MODEL
    return pl.pallas_call(
        _ada_kernel,
        out_shape=jax.ShapeDtypeStruct((B, n_out), f32),
        grid=(n_out // bn,),
        in_specs=[pl.BlockSpec((B, D_MODEL), lambda j: (0, 0)),
                  pl.BlockSpec((D_MODEL, bn), lambda j: (0, j)),
                  pl.BlockSpec((1, bn), lambda j: (0, j))],
        out_specs=pl.BlockSpec((B, bn), lambda j: (0, j)),
        name="ada",
    )(c, w_ada, b_ada.reshape(1, n_out))


def _rope_kernel(pos_ref, f_ref, cos_ref, sin_ref):
    ang = pos_ref[...].astype(f32) * f_ref[...]
    cos_ref[...] = jnp.cos(ang)
    sin_ref[...] = jnp.sin(ang)


def _rope_tables(pos_rep, freq_row):
    rows = pos_rep.shape[0]
    br = 512
    spec = pl.BlockSpec((br, 128), lambda i: (i, 0))
    return pl.pallas_call(
        _rope_kernel,
        out_shape=(jax.ShapeDtypeStruct((rows, 128), f32),) * 2,
        grid=(rows // br,),
        in_specs=[spec, pl.BlockSpec((1, 128), lambda i: (0, 0))],
        out_specs=(spec, spec),
        name="rope_tables",
    )(pos_rep, freq_row)


def _ffn1_kernel(x_ref, mod_ref, lng_ref, lnb_ref, w_in_ref, w_out_ref, pg_ref, pb_ref, o_ref):
    x = _layernorm(x_ref[0], lng_ref[...], lnb_ref[...])
    shift, scale, gate = mod_ref[0, 0:1, :], mod_ref[0, 1:2, :], mod_ref[0, 2:3, :]
    h = (x * (1.0 + scale) + shift).astype(bf16)
    y = _swiglu(h, w_in_ref, w_out_ref)
    o_ref[0] = _layernorm(ALPHA * x + 0.5 * gate * y, pg_ref[...], pb_ref[...])


def _ffn2_kernel(x_ref, attn_ref, yp_ref, mod_ref, woa_ref, wop_ref, pg1_ref, pb1_ref,
                 w_in_ref, w_out_ref, pg2_ref, pb2_ref, o_ref):
    gate1 = mod_ref[0, 5:6, :]
    y = (jnp.dot(attn_ref[0], woa_ref[...], preferred_element_type=f32)
         + jnp.dot(yp_ref[0], wop_ref[...], preferred_element_type=f32))
    x = _layernorm(ALPHA * x_ref[0] + gate1 * y, pg1_ref[...], pb1_ref[...])
    shift, scale, gate = mod_ref[0, 6:7, :], mod_ref[0, 7:8, :], mod_ref[0, 8:9, :]
    h = (x * (1.0 + scale) + shift).astype(bf16)
    y = _swiglu(h, w_in_ref, w_out_ref)
    o_ref[0] = _layernorm(ALPHA * x + 0.5 * gate * y, pg2_ref[...], pb2_ref[...])


def _resident(shape):
    return pl.BlockSpec(shape, lambda b, i: (0,) * len(shape), pipeline_mode=pl.Buffered(1))


def _row_spec(tm, width):
    return pl.BlockSpec((1, tm, width), lambda b, i: (b, i, 0))


def _mod_spec():
    return pl.BlockSpec((1, 9, D_MODEL), lambda b, i: (b, 0, 0))


def _ffn1(x, mod, lng, lnb, w_in, w_out, pg, pb):
    B, S, D = x.shape
    vec = _resident((1, D))
    return pl.pallas_call(
        _ffn1_kernel,
        out_shape=jax.ShapeDtypeStruct((B, S, D), f32),
        grid=(B, S // TM_FFN),
        in_specs=[_row_spec(TM_FFN, D), _mod_spec(), vec, vec,
                  _resident(w_in.shape), _resident(w_out.shape), vec, vec],
        out_specs=_row_spec(TM_FFN, D),
        compiler_params=pltpu.CompilerParams(
            dimension_semantics=("parallel", "parallel"), vmem_limit_bytes=VMEM_LIMIT),
        name="ffn1",
    )(x, mod, lng, lnb, w_in, w_out, pg, pb)


def _ffn2(x1, attn, yp, mod, woa, wop, pg1, pb1, w_in, w_out, pg2, pb2):
    B, S, D = x1.shape
    vec = _resident((1, D))
    return pl.pallas_call(
        _ffn2_kernel,
        out_shape=jax.ShapeDtypeStruct((B, S, D), f32),
        grid=(B, S // TM_FFN),
        in_specs=[_row_spec(TM_FFN, D), _row_spec(TM_FFN, attn.shape[-1]),
                  _row_spec(TM_FFN, yp.shape[-1]), _mod_spec(),
                  _resident(woa.shape), _resident(wop.shape), vec, vec,
                  _resident(w_in.shape), _resident(w_out.shape), vec, vec],
        out_specs=_row_spec(TM_FFN, D),
        compiler_params=pltpu.CompilerParams(
            dimension_semantics=("parallel", "parallel"), vmem_limit_bytes=VMEM_LIMIT),
        name="ffn2",
    )(x1, attn, yp, mod, woa, wop, pg1, pb1, w_in, w_out, pg2, pb2)


def _apply_rope(t, cos_t, sin_t):
    return t * cos_t + pltpu.roll(t, HEAD_PAD - QK_ROPE, 1) * sin_t


def _mix_kernel(x_ref, mod_ref, cos_ref, sin_ref, w_in_ref, qg_ref, kvg_ref, wuq_ref, wuk_ref,
                wuvt_ref, wpool_ref, pbias_ref, pscale_ref,
                q_ref, k_ref, vt_ref, yp_ref, tail_ref):
    st = pl.program_id(1)
    tm = x_ref.shape[1]
    shift, scale = mod_ref[0, 3:4, :], mod_ref[0, 4:5, :]
    h = (x_ref[0] * (1.0 + scale) + shift).astype(bf16)
    proj = jnp.dot(h, w_in_ref[...], preferred_element_type=f32)
    cq = proj[:, :Q_LORA]
    ckv = proj[:, Q_LORA:Q_LORA + KV_LORA]
    krb = proj[:, Q_LORA + KV_LORA:Q_LORA + KV_LORA + HEAD_PAD]
    u = proj[:, Q_LORA + KV_LORA + HEAD_PAD:]

    cos_t, sin_t = cos_ref[0], sin_ref[0]

    cqn = _rmsnorm(cq, qg_ref[...]).astype(bf16)
    q_all = jnp.dot(cqn, wuq_ref[...], preferred_element_type=f32)
    ckvn = _rmsnorm(ckv, kvg_ref[...]).astype(bf16)
    k_all = jnp.dot(ckvn, wuk_ref[...], preferred_element_type=f32)
    v_t = lax.dot_general(wuvt_ref[...], ckvn, _NT, preferred_element_type=f32).astype(bf16)
    ones_rows = (lax.broadcasted_iota(jnp.int32, (V_SLAB - V_DIM, TK), 0) == 0).astype(bf16)
    for t in range(tm // TK):
        for hh in range(N_HEADS):
            vt_ref[0, t, hh * V_SLAB:hh * V_SLAB + V_DIM, :] = (
                v_t[hh * V_DIM:(hh + 1) * V_DIM, t * TK:(t + 1) * TK])
            vt_ref[0, t, hh * V_SLAB + V_DIM:(hh + 1) * V_SLAB, :] = ones_rows
    k_rope = _apply_rope(krb, cos_t, sin_t)
    for hh in range(N_HEADS):
        sl = slice(hh * HEAD_PAD, (hh + 1) * HEAD_PAD)
        q_ref[0, :, sl] = (_apply_rope(q_all[:, sl], cos_t, sin_t) * (SM_SCALE * LOG2_E)).astype(bf16)
        k_ref[0, :, sl] = (k_all[:, sl] + k_rope).astype(bf16)

    @pl.when(st == 0)
    def _():
        tail_ref[...] = jnp.zeros_like(tail_ref)

    ext = jnp.concatenate([tail_ref[...], u], axis=0)
    tail_ref[...] = u[tm - POOL_HALO:, :]
    t_idx = st * tm + lax.broadcasted_iota(jnp.int32, (tm, 1), 0)
    groups = []
    for gi, w in enumerate(POOL_WINDOWS):
        e = ext[:, gi * POOL_GROUP:(gi + 1) * POOL_GROUP]
        sh = 1
        while sh < w:
            e = e + pltpu.roll(e, sh, 0)
            sh *= 2
        cnt = jnp.minimum(t_idx + 1, w).astype(f32)
        groups.append(e[POOL_HALO:, :] / cnt - u[:, gi * POOL_GROUP:(gi + 1) * POOL_GROUP])
    pooled = jnp.concatenate(groups, axis=1).astype(bf16)
    yp = jnp.dot(pooled, wpool_ref[...], preferred_element_type=f32) + pbias_ref[...]
    yp_ref[0] = (yp * pscale_ref[...]).astype(bf16)


def _mix(x1, mod, cos_t, sin_t, w_in, qg, kvg, wuq, wuk, wuvt, wpool, pbias, pscale):
    B, S, D = x1.shape
    tm = TM_MIX
    hw = N_HEADS * HEAD_PAD
    vw = N_HEADS * V_SLAB

    def const(shape):
        return pl.BlockSpec(shape, lambda b, i: (0,) * len(shape))

    return pl.pallas_call(
        _mix_kernel,
        out_shape=(jax.ShapeDtypeStruct((B, S, hw), bf16),
                   jax.ShapeDtypeStruct((B, S, hw), bf16),
                   jax.ShapeDtypeStruct((B, S // TK, vw, TK), bf16),
                   jax.ShapeDtypeStruct((B, S, POOL_WIDTH), bf16)),
        grid=(B, S // tm),
        in_specs=[_row_spec(tm, D), _mod_spec(), _row_spec(tm, HEAD_PAD), _row_spec(tm, HEAD_PAD),
                  const(w_in.shape), const(qg.shape), const(kvg.shape), const(wuq.shape),
                  const(wuk.shape), const(wuvt.shape), const(wpool.shape), const(pbias.shape),
                  const(pscale.shape)],
        out_specs=(_row_spec(tm, hw), _row_spec(tm, hw),
                   pl.BlockSpec((1, tm // TK, vw, TK), lambda b, i: (b, i, 0, 0)),
                   _row_spec(tm, POOL_WIDTH)),
        scratch_shapes=[pltpu.VMEM((POOL_HALO, POOL_WIDTH), f32)],
        compiler_params=pltpu.CompilerParams(
            dimension_semantics=("parallel", "arbitrary"), vmem_limit_bytes=VMEM_LIMIT),
        name="mix_proj",
    )(x1, mod, cos_t, sin_t, w_in, qg, kvg, wuq, wuk, wuvt, wpool, pbias, pscale)


def _attn_kernel(q_ref, k_ref, vt_ref, o_ref, m_ref, acc_ref):
    i = pl.program_id(1)
    m_ref[...] = jnp.full_like(m_ref, -jnp.inf)
    acc_ref[...] = jnp.zeros_like(acc_ref)

    key_chunk = lax.broadcasted_iota(jnp.int32, (TK, TQ), 0) // CHUNK
    qry_chunk = lax.broadcasted_iota(jnp.int32, (TK, TQ), 1) // CHUNK
    diag_mask = key_chunk <= qry_chunk

    def step(j, masked):
        koff = pl.multiple_of(j * TK, TK)

        def scores(hh):
            sl = slice(hh * HEAD_PAD, (hh + 1) * HEAD_PAD)
            return lax.dot_general(k_ref[0, pl.ds(koff, TK), sl], q_ref[0, :, sl], _NT,
                                   preferred_element_type=f32)

        s_all = [scores(hh) for hh in range(N_HEADS)]
        for hh in range(N_HEADS):
            s = s_all[hh]
            if masked:
                s = jnp.where(diag_mask, s, -jnp.inf)
            m_prev = m_ref[hh]
            m_new = jnp.maximum(m_prev, jnp.max(s, axis=0, keepdims=True))
            alpha = jnp.exp2(m_prev - m_new)
            p = jnp.exp2(s - m_new).astype(bf16)
            v_t = vt_ref[0, j, hh * V_SLAB:(hh + 1) * V_SLAB, :]
            acc_ref[hh] = alpha * acc_ref[hh] + jnp.dot(v_t, p, preferred_element_type=f32)
            m_ref[hh] = m_new

    def body(j, carry):
        step(j, False)
        return carry

    lax.fori_loop(0, i, body, 0)
    step(i, True)
    o_t = jnp.concatenate([acc_ref[hh, :V_DIM, :] / acc_ref[hh, V_DIM:V_DIM + 1, :]
                           for hh in range(N_HEADS)], axis=0)
    o_ref[0] = o_t.T.astype(bf16)


def _attention(q, k, v_t):
    B, S, hw = q.shape
    vw = N_HEADS * V_DIM
    return pl.pallas_call(
        _attn_kernel,
        out_shape=jax.ShapeDtypeStruct((B, S, vw), bf16),
        grid=(B, S // TQ),
        in_specs=[_row_spec(TQ, hw),
                  pl.BlockSpec((1, S, hw), lambda b, i: (b, 0, 0)),
                  pl.BlockSpec((1,) + v_t.shape[1:], lambda b, i: (b, 0, 0, 0))],
        out_specs=_row_spec(TQ, vw),
        scratch_shapes=[pltpu.VMEM((N_HEADS, 1, TQ), f32),
                        pltpu.VMEM((N_HEADS, V_SLAB, TQ), f32)],
        compiler_params=pltpu.CompilerParams(
            dimension_semantics=("parallel", "parallel"), vmem_limit_bytes=VMEM_LIMIT),
        name="attention",
    )(q, k, v_t)


def _rot_cols(w):
    half = w.shape[-1] // 2
    return jnp.concatenate([-w[..., half:], w[..., :half]], axis=-1)


def _head_slots(parts):
    k_dim = parts[0][0].shape[0]
    cols = []
    for blocks in parts:
        used = sum(b.shape[1] for b in blocks)
        cols.extend(blocks)
        if used < HEAD_PAD:
            cols.append(jnp.zeros((k_dim, HEAD_PAD - used), f32))
    return jnp.concatenate(cols, axis=1)


def kernel(x, c, positions, ln_in_g, ln_in_b, w_ada, b_ada, ffn1_w_in, ffn1_w_out, w_in, q_norm_g, kv_norm_g, w_uq, w_ukv, pool_w, pool_b, pool_scale, w_out, ffn2_w_in, ffn2_w_out, post_ln_g, post_ln_b):
    B, S, D = x.shape
    l = 0
    row = lambda a: a.reshape(1, -1)

    qh = QK_NOPE + QK_ROPE
    wuq = w_uq[l]
    wuq_pad = _head_slots([[wuq[:, h * qh:h * qh + QK_NOPE],
                            wuq[:, h * qh + QK_NOPE:(h + 1) * qh],
                            _rot_cols(wuq[:, h * qh + QK_NOPE:(h + 1) * qh])]
                           for h in range(N_HEADS)]).astype(bf16)
    kvh = QK_NOPE + V_DIM
    wukv = w_ukv[l]
    wuk_pad = _head_slots([[wukv[:, h * kvh:h * kvh + QK_NOPE]] for h in range(N_HEADS)]).astype(bf16)
    wuv_t = jnp.concatenate([wukv[:, h * kvh + QK_NOPE:(h + 1) * kvh] for h in range(N_HEADS)],
                            axis=1).T.astype(bf16)
    w_kr = w_in[l][:, Q_LORA + KV_LORA:Q_LORA + KV_LORA + QK_ROPE]
    w_in_pad = jnp.concatenate(
        [w_in[l][:, :Q_LORA + KV_LORA], jnp.zeros((D, QK_NOPE), f32), w_kr, _rot_cols(w_kr),
         w_in[l][:, Q_LORA + KV_LORA + QK_ROPE:]], axis=1).astype(bf16)
    wpool = jax.scipy.linalg.block_diag(*[pool_w[l, g] for g in range(len(POOL_WINDOWS))]).astype(bf16)
    wo = w_out[l]
    woa = wo[:N_HEADS * V_DIM].astype(bf16)
    wop = wo[N_HEADS * V_DIM:].astype(bf16)

    inv_freq = ROPE_THETA ** (-jnp.arange(0, QK_ROPE, 2, dtype=f32) / QK_ROPE)
    pos_rep = jnp.repeat(positions.reshape(B * S // 8, 8), QK_ROPE // 2, axis=1)
    cos_c, sin_c = _rope_tables(pos_rep, jnp.tile(inv_freq, 8).reshape(1, 128))
    cos_c = cos_c.reshape(B, S, QK_ROPE // 2)
    sin_c = sin_c.reshape(B, S, QK_ROPE // 2)
    cos_t = jnp.concatenate([jnp.ones((B, S, QK_NOPE), f32), cos_c, cos_c,
                             jnp.zeros((B, S, QK_ROPE), f32)], axis=-1)
    sin_t = jnp.concatenate([jnp.zeros((B, S, QK_NOPE), f32), sin_c, sin_c,
                             jnp.zeros((B, S, QK_ROPE), f32)], axis=-1)

    mod = _ada(c, w_ada[l], b_ada[l]).reshape(B, 9, D)

    x1 = _ffn1(x, mod, row(ln_in_g), row(ln_in_b), ffn1_w_in[l].astype(bf16),
               ffn1_w_out[l].astype(bf16), row(post_ln_g[l, 0]), row(post_ln_b[l, 0]))
    q, k, v_t, yp = _mix(x1, mod, cos_t, sin_t, w_in_pad, row(q_norm_g[l]), row(kv_norm_g[l]),
                         wuq_pad, wuk_pad, wuv_t, wpool, row(pool_b[l]), row(pool_scale[l]))
    attn = _attention(q, k, v_t)
    return _ffn2(x1, attn, yp, mod, woa, wop, row(post_ln_g[l, 1]), row(post_ln_b[l, 1]),
                 ffn2_w_in[l].astype(bf16), ffn2_w_out[l].astype(bf16),
                 row(post_ln_g[l, 2]), row(post_ln_b[l, 2]))
```

```python
import jax
import jax.numpy as jnp
from jax import lax
from jax.experimental import pallas as pl
from jax.experimental.pallas import tpu as pltpu

f32 = jnp.float32
bf16 = jnp.bfloat16

D_MODEL = 1024
D_FF = 2816
N_HEADS = 8
QK_NOPE = 64
QK_ROPE = 32
V_DIM = 64
Q_LORA = 512
KV_LORA = 256
HEAD_PAD = 128
POOL_WINDOWS = (2, 4, 8, 16)
POOL_GROUP = 128
POOL_WIDTH = 512
POOL_HALO = 16
CHUNK = 64
ROPE_THETA = 10000.0
LN_EPS = 1e-5
RMS_EPS = 1e-6
ALPHA = 2.0 ** 0.25
SM_SCALE = (QK_NOPE + QK_ROPE) ** -0.5
LOG2_E = 1.4426950408889634
V_SLAB = 80
IN_PAD = Q_LORA + KV_LORA + HEAD_PAD + POOL_WIDTH

TM_FFN = 512
TM_MIX = 512
TQ = 256
TK = 256
FF_CHUNKS = (1024, 1024, 768)
VMEM_LIMIT = 56 * 1024 * 1024

_NT = (((1,), (1,)), ((), ()))


def _layernorm(x, g, b):
    mu = jnp.mean(x, axis=-1, keepdims=True)
    xc = x - mu
    var = jnp.mean(xc * xc, axis=-1, keepdims=True)
    return xc * lax.rsqrt(var + LN_EPS) * g + b


def _rmsnorm(x, g):
    ms = jnp.mean(x * x, axis=-1, keepdims=True)
    return x * lax.rsqrt(ms + RMS_EPS) * g


def _swiglu(h, w_in_ref, w_out_ref):
    y = None
    off = 0
    for n in FF_CHUNKS:
        g = jnp.dot(h, w_in_ref[:, off:off + n], preferred_element_type=f32)
        u = jnp.dot(h, w_in_ref[:, D_FF + off:D_FF + off + n], preferred_element_type=f32)
        a = (g / (1.0 + jnp.exp(-g)) * u).astype(bf16)
        part = jnp.dot(a, w_out_ref[off:off + n, :], preferred_element_type=f32)
        y = part if y is None else y + part
        off += n
    return y


def _ada_kernel(c_ref, w_ref, b_ref, o_ref):
    c = c_ref[...]
    c_act = c / (1.0 + jnp.exp(-c))
    o_ref[...] = jnp.dot(c_act, w_ref[...], preferred_element_type=f32,
                         precision=lax.Precision.HIGHEST) + b_ref[...]


def _ada(c, w_ada, b_ada):
    B = c.shape[0]
    n_out = w_ada.shape[1]
    bn = D_MODEL
    return pl.pallas_call(
        _ada_kernel,
        out_shape=jax.ShapeDtypeStruct((B, n_out), f32),
        grid=(n_out // bn,),
        in_specs=[pl.BlockSpec((B, D_MODEL), lambda j: (0, 0)),
                  pl.BlockSpec((D_MODEL, bn), lambda j: (0, j)),
                  pl.BlockSpec((1, bn), lambda j: (0, j))],
        out_specs=pl.BlockSpec((B, bn), lambda j: (0, j)),
        name="ada",
    )(c, w_ada, b_ada.reshape(1, n_out))


def _rope_kernel(pos_ref, f_ref, cos_ref, sin_ref):
    ang = pos_ref[...].astype(f32) * f_ref[...]
    cos_ref[...] = jnp.cos(ang)
    sin_ref[...] = jnp.sin(ang)


def _rope_tables(pos_rep, freq_row):
    rows = pos_rep.shape[0]
    br = 512
    spec = pl.BlockSpec((br, 128), lambda i: (i, 0))
    return pl.pallas_call(
        _rope_kernel,
        out_shape=(jax.ShapeDtypeStruct((rows, 128), f32),) * 2,
        grid=(rows // br,),
        in_specs=[spec, pl.BlockSpec((1, 128), lambda i: (0, 0))],
        out_specs=(spec, spec),
        name="rope_tables",
    )(pos_rep, freq_row)


def _ffn1_kernel(x_ref, mod_ref, lng_ref, lnb_ref, w_in_ref, w_out_ref, pg_ref, pb_ref, o_ref):
    x = _layernorm(x_ref[0], lng_ref[...], lnb_ref[...])
    shift, scale, gate = mod_ref[0, 0:1, :], mod_ref[0, 1:2, :], mod_ref[0, 2:3, :]
    h = (x * (1.0 + scale) + shift).astype(bf16)
    y = _swiglu(h, w_in_ref, w_out_ref)
    o_ref[0] = _layernorm(ALPHA * x + 0.5 * gate * y, pg_ref[...], pb_ref[...])


def _ffn2_kernel(x_ref, attn_ref, yp_ref, mod_ref, woa_ref, wop_ref, pg1_ref, pb1_ref,
                 w_in_ref, w_out_ref, pg2_ref, pb2_ref, o_ref):
    gate1 = mod_ref[0, 5:6, :]
    y = (jnp.dot(attn_ref[0], woa_ref[...], preferred_element_type=f32)
         + jnp.dot(yp_ref[0], wop_ref[...], preferred_element_type=f32))
    x = _layernorm(ALPHA * x_ref[0] + gate1 * y, pg1_ref[...], pb1_ref[...])
    shift, scale, gate = mod_ref[0, 6:7, :], mod_ref[0, 7:8, :], mod_ref[0, 8:9, :]
    h = (x * (1.0 + scale) + shift).astype(bf16)
    y = _swiglu(h, w_in_ref, w_out_ref)
    o_ref[0] = _layernorm(ALPHA * x + 0.5 * gate * y, pg2_ref[...], pb2_ref[...])


def _resident(shape):
    return pl.BlockSpec(shape, lambda b, i: (0,) * len(shape), pipeline_mode=pl.Buffered(1))


def _row_spec(tm, width):
    return pl.BlockSpec((1, tm, width), lambda b, i: (b, i, 0))


def _mod_spec():
    return pl.BlockSpec((1, 9, D_MODEL), lambda b, i: (b, 0, 0))


def _ffn1(x, mod, lng, lnb, w_in, w_out, pg, pb):
    B, S, D = x.shape
    vec = _resident((1, D))
    return pl.pallas_call(
        _ffn1_kernel,
        out_shape=jax.ShapeDtypeStruct((B, S, D), f32),
        grid=(B, S // TM_FFN),
        in_specs=[_row_spec(TM_FFN, D), _mod_spec(), vec, vec,
                  _resident(w_in.shape), _resident(w_out.shape), vec, vec],
        out_specs=_row_spec(TM_FFN, D),
        compiler_params=pltpu.CompilerParams(
            dimension_semantics=("parallel", "parallel"), vmem_limit_bytes=VMEM_LIMIT),
        name="ffn1",
    )(x, mod, lng, lnb, w_in, w_out, pg, pb)


def _ffn2(x1, attn, yp, mod, woa, wop, pg1, pb1, w_in, w_out, pg2, pb2):
    B, S, D = x1.shape
    vec = _resident((1, D))
    return pl.pallas_call(
        _ffn2_kernel,
        out_shape=jax.ShapeDtypeStruct((B, S, D), f32),
        grid=(B, S // TM_FFN),
        in_specs=[_row_spec(TM_FFN, D), _row_spec(TM_FFN, attn.shape[-1]),
                  _row_spec(TM_FFN, yp.shape[-1]), _mod_spec(),
                  _resident(woa.shape), _resident(wop.shape), vec, vec,
                  _resident(w_in.shape), _resident(w_out.shape), vec, vec],
        out_specs=_row_spec(TM_FFN, D),
        compiler_params=pltpu.CompilerParams(
            dimension_semantics=("parallel", "parallel"), vmem_limit_bytes=VMEM_LIMIT),
        name="ffn2",
    )(x1, attn, yp, mod, woa, wop, pg1, pb1, w_in, w_out, pg2, pb2)


def _apply_rope(t, cos_t, sin_t):
    return t * cos_t + pltpu.roll(t, HEAD_PAD - QK_ROPE, 1) * sin_t


def _mix_kernel(x_ref, mod_ref, cos_ref, sin_ref, w_in_ref, qg_ref, kvg_ref, wuq_ref, wuk_ref,
                wuvt_ref, wpool_ref, pbias_ref, pscale_ref,
                q_ref, k_ref, vt_ref, yp_ref, tail_ref):
    st = pl.program_id(1)
    tm = x_ref.shape[1]
    shift, scale = mod_ref[0, 3:4, :], mod_ref[0, 4:5, :]
    h = (x_ref[0] * (1.0 + scale) + shift).astype(bf16)
    proj = jnp.dot(h, w_in_ref[...], preferred_element_type=f32)
    cq = proj[:, :Q_LORA]
    ckv = proj[:, Q_LORA:Q_LORA + KV_LORA]
    krb = proj[:, Q_LORA + KV_LORA:Q_LORA + KV_LORA + HEAD_PAD]
    u = proj[:, Q_LORA + KV_LORA + HEAD_PAD:]

    cos_t, sin_t = cos_ref[0], sin_ref[0]

    cqn = _rmsnorm(cq, qg_ref[...]).astype(bf16)
    q_all = jnp.dot(cqn, wuq_ref[...], preferred_element_type=f32)
    ckvn = _rmsnorm(ckv, kvg_ref[...]).astype(bf16)
    k_all = jnp.dot(ckvn, wuk_ref[...], preferred_element_type=f32)
    v_t = lax.dot_general(wuvt_ref[...], ckvn, _NT, preferred_element_type=f32).astype(bf16)
    ones_rows = (lax.broadcasted_iota(jnp.int32, (V_SLAB - V_DIM, TK), 0) == 0).astype(bf16)
    for t in range(tm // TK):
        for hh in range(N_HEADS):
            vt_ref[0, t, hh * V_SLAB:hh * V_SLAB + V_DIM, :] = (
                v_t[hh * V_DIM:(hh + 1) * V_DIM, t * TK:(t + 1) * TK])
            vt_ref[0, t, hh * V_SLAB + V_DIM:(hh + 1) * V_SLAB, :] = ones_rows
    k_rope = _apply_rope(krb, cos_t, sin_t)
    for hh in range(N_HEADS):
        sl = slice(hh * HEAD_PAD, (hh + 1) * HEAD_PAD)
        q_ref[0, :, sl] = (_apply_rope(q_all[:, sl], cos_t, sin_t) * (SM_SCALE * LOG2_E)).astype(bf16)
        k_ref[0, :, sl] = (k_all[:, sl] + k_rope).astype(bf16)

    @pl.when(st == 0)
    def _():
        tail_ref[...] = jnp.zeros_like(tail_ref)

    ext = jnp.concatenate([tail_ref[...], u], axis=0)
    tail_ref[...] = u[tm - POOL_HALO:, :]
    t_idx = st * tm + lax.broadcasted_iota(jnp.int32, (tm, 1), 0)
    groups = []
    for gi, w in enumerate(POOL_WINDOWS):
        e = ext[:, gi * POOL_GROUP:(gi + 1) * POOL_GROUP]
        sh = 1
        while sh < w:
            e = e + pltpu.roll(e, sh, 0)
            sh *= 2
        cnt = jnp.minimum(t_idx + 1, w).astype(f32)
        groups.append(e[POOL_HALO:, :] / cnt - u[:, gi * POOL_GROUP:(gi + 1) * POOL_GROUP])
    pooled = jnp.concatenate(groups, axis=1).astype(bf16)
    yp = jnp.dot(pooled, wpool_ref[...], preferred_element_type=f32) + pbias_ref[...]
    yp_ref[0] = (yp * pscale_ref[...]).astype(bf16)


def _mix(x1, mod, cos_t, sin_t, w_in, qg, kvg, wuq, wuk, wuvt, wpool, pbias, pscale):
    B, S, D = x1.shape
    tm = TM_MIX
    hw = N_HEADS * HEAD_PAD
    vw = N_HEADS * V_SLAB

    def const(shape):
        return pl.BlockSpec(shape, lambda b, i: (0,) * len(shape))

    return pl.pallas_call(
        _mix_kernel,
        out_shape=(jax.ShapeDtypeStruct((B, S, hw), bf16),
                   jax.ShapeDtypeStruct((B, S, hw), bf16),
                   jax.ShapeDtypeStruct((B, S // TK, vw, TK), bf16),
                   jax.ShapeDtypeStruct((B, S, POOL_WIDTH), bf16)),
        grid=(B, S // tm),
        in_specs=[_row_spec(tm, D), _mod_spec(), _row_spec(tm, HEAD_PAD), _row_spec(tm, HEAD_PAD),
                  const(w_in.shape), const(qg.shape), const(kvg.shape), const(wuq.shape),
                  const(wuk.shape), const(wuvt.shape), const(wpool.shape), const(pbias.shape),
                  const(pscale.shape)],
        out_specs=(_row_spec(tm, hw), _row_spec(tm, hw),
                   pl.BlockSpec((1, tm // TK, vw, TK), lambda b, i: (b, i, 0, 0)),
                   _row_spec(tm, POOL_WIDTH)),
        scratch_shapes=[pltpu.VMEM((POOL_HALO, POOL_WIDTH), f32)],
        compiler_params=pltpu.CompilerParams(
            dimension_semantics=("parallel", "arbitrary"), vmem_limit_bytes=VMEM_LIMIT),
        name="mix_proj",
    )(x1, mod, cos_t, sin_t, w_in, qg, kvg, wuq, wuk, wuvt, wpool, pbias, pscale)


def _attn_kernel(q_ref, k_ref, vt_ref, o_ref, m_ref, acc_ref, s_a, s_b, mt_a, mt_b):
    i = pl.program_id(1)
    m_ref[...] = jnp.full_like(m_ref, -jnp.inf)
    acc_ref[...] = jnp.zeros_like(acc_ref)

    def qk_stage(j, s_buf, mt_buf, masked):
        koff = pl.multiple_of(j * TK, TK)
        for hh in range(N_HEADS):
            sl = slice(hh * HEAD_PAD, (hh + 1) * HEAD_PAD)
            s = lax.dot_general(k_ref[0, pl.ds(koff, TK), sl], q_ref[0, :, sl], _NT,
                                preferred_element_type=f32)
            if masked:
                key_chunk = lax.broadcasted_iota(jnp.int32, (TK, TQ), 0) // CHUNK
                qry_chunk = lax.broadcasted_iota(jnp.int32, (TK, TQ), 1) // CHUNK
                s = jnp.where(key_chunk <= qry_chunk, s, -jnp.inf)
            s_buf[hh] = s
            mt_buf[hh] = jnp.max(s, axis=0, keepdims=True)

    def pv_stage(j, s_buf, mt_buf):
        for hh in range(N_HEADS):
            m_prev = m_ref[hh]
            m_new = jnp.maximum(m_prev, mt_buf[hh])
            alpha = jnp.exp2(m_prev - m_new)
            p = jnp.exp2(s_buf[hh] - m_new).astype(bf16)
            v_t = vt_ref[0, j, hh * V_SLAB:(hh + 1) * V_SLAB, :]
            acc_ref[hh] = alpha * acc_ref[hh] + jnp.dot(v_t, p, preferred_element_type=f32)
            m_ref[hh] = m_new

    qk_stage(i, s_a, mt_a, True)
    n_pairs = i // 2

    def body(u, carry):
        qk_stage(2 * u, s_b, mt_b, False)
        pv_stage(jnp.where(u == 0, i, 2 * u - 1), s_a, mt_a)
        qk_stage(2 * u + 1, s_a, mt_a, False)
        pv_stage(2 * u, s_b, mt_b)
        return carry

    lax.fori_loop(0, n_pairs, body, 0)
    tile_a = jnp.where(n_pairs == 0, i, 2 * n_pairs - 1)

    @pl.when(i % 2 == 1)
    def _():
        qk_stage(i - 1, s_b, mt_b, False)
        pv_stage(tile_a, s_a, mt_a)
        pv_stage(i - 1, s_b, mt_b)

    @pl.when(i % 2 == 0)
    def _():
        pv_stage(tile_a, s_a, mt_a)

    o_t = jnp.concatenate([acc_ref[hh, :V_DIM, :] / acc_ref[hh, V_DIM:V_DIM + 1, :]
                           for hh in range(N_HEADS)], axis=0)
    o_ref[0] = o_t.T.astype(bf16)


def _attention(q, k, v_t):
    B, S, hw = q.shape
    vw = N_HEADS * V_DIM
    return pl.pallas_call(
        _attn_kernel,
        out_shape=jax.ShapeDtypeStruct((B, S, vw), bf16),
        grid=(B, S // TQ),
        in_specs=[_row_spec(TQ, hw),
                  pl.BlockSpec((1, S, hw), lambda b, i: (b, 0, 0)),
                  pl.BlockSpec((1,) + v_t.shape[1:], lambda b, i: (b, 0, 0, 0))],
        out_specs=_row_spec(TQ, vw),
        scratch_shapes=[pltpu.VMEM((N_HEADS, 1, TQ), f32),
                        pltpu.VMEM((N_HEADS, V_SLAB, TQ), f32),
                        pltpu.VMEM((N_HEADS, TK, TQ), f32),
                        pltpu.VMEM((N_HEADS, TK, TQ), f32),
                        pltpu.VMEM((N_HEADS, 1, TQ), f32),
                        pltpu.VMEM((N_HEADS, 1, TQ), f32)],
        compiler_params=pltpu.CompilerParams(
            dimension_semantics=("parallel", "parallel"), vmem_limit_bytes=VMEM_LIMIT),
        name="attention",
    )(q, k, v_t)


def _rot_cols(w):
    half = w.shape[-1] // 2
    return jnp.concatenate([-w[..., half:], w[..., :half]], axis=-1)


def _head_slots(parts):
    k_dim = parts[0][0].shape[0]
    cols = []
    for blocks in parts:
        used = sum(b.shape[1] for b in blocks)
        cols.extend(blocks)
        if used < HEAD_PAD:
            cols.append(jnp.zeros((k_dim, HEAD_PAD - used), f32))
    return jnp.concatenate(cols, axis=1)


def kernel(x, c, positions, ln_in_g, ln_in_b, w_ada, b_ada, ffn1_w_in, ffn1_w_out, w_in, q_norm_g, kv_norm_g, w_uq, w_ukv, pool_w, pool_b, pool_scale, w_out, ffn2_w_in, ffn2_w_out, post_ln_g, post_ln_b):
    B, S, D = x.shape
    l = 0
    row = lambda a: a.reshape(1, -1)

    qh = QK_NOPE + QK_ROPE
    wuq = w_uq[l]
    wuq_pad = _head_slots([[wuq[:, h * qh:h * qh + QK_NOPE],
                            wuq[:, h * qh + QK_NOPE:(h + 1) * qh],
                            _rot_cols(wuq[:, h * qh + QK_NOPE:(h + 1) * qh])]
                           for h in range(N_HEADS)]).astype(bf16)
    kvh = QK_NOPE + V_DIM
    wukv = w_ukv[l]
    wuk_pad = _head_slots([[wukv[:, h * kvh:h * kvh + QK_NOPE]] for h in range(N_HEADS)]).astype(bf16)
    wuv_t = jnp.concatenate([wukv[:, h * kvh + QK_NOPE:(h + 1) * kvh] for h in range(N_HEADS)],
                            axis=1).T.astype(bf16)
    w_kr = w_in[l][:, Q_LORA + KV_LORA:Q_LORA + KV_LORA + QK_ROPE]
    w_in_pad = jnp.concatenate(
        [w_in[l][:, :Q_LORA + KV_LORA], jnp.zeros((D, QK_NOPE), f32), w_kr, _rot_cols(w_kr),
         w_in[l][:, Q_LORA + KV_LORA + QK_ROPE:]], axis=1).astype(bf16)
    wpool = jax.scipy.linalg.block_diag(*[pool_w[l, g] for g in range(len(POOL_WINDOWS))]).astype(bf16)
    wo = w_out[l]
    woa = wo[:N_HEADS * V_DIM].astype(bf16)
    wop = wo[N_HEADS * V_DIM:].astype(bf16)

    inv_freq = ROPE_THETA ** (-jnp.arange(0, QK_ROPE, 2, dtype=f32) / QK_ROPE)
    pos_rep = jnp.repeat(positions.reshape(B * S // 8, 8), QK_ROPE // 2, axis=1)
    cos_c, sin_c = _rope_tables(pos_rep, jnp.tile(inv_freq, 8).reshape(1, 128))
    cos_c = cos_c.reshape(B, S, QK_ROPE // 2)
    sin_c = sin_c.reshape(B, S, QK_ROPE // 2)
    cos_t = jnp.concatenate([jnp.ones((B, S, QK_NOPE), f32), cos_c, cos_c,
                             jnp.zeros((B, S, QK_ROPE), f32)], axis=-1)
    sin_t = jnp.concatenate([jnp.zeros((B, S, QK_NOPE), f32), sin_c, sin_c,
                             jnp.zeros((B, S, QK_ROPE), f32)], axis=-1)

    mod = _ada(c, w_ada[l], b_ada[l]).reshape(B, 9, D)

    x1 = _ffn1(x, mod, row(ln_in_g), row(ln_in_b), ffn1_w_in[l].astype(bf16),
               ffn1_w_out[l].astype(bf16), row(post_ln_g[l, 0]), row(post_ln_b[l, 0]))
    q, k, v_t, yp = _mix(x1, mod, cos_t, sin_t, w_in_pad, row(q_norm_g[l]), row(kv_norm_g[l]),
                         wuq_pad, wuk_pad, wuv_t, wpool, row(pool_b[l]), row(pool_scale[l]))
    attn = _attention(q, k, v_t)
    return _ffn2(x1, attn, yp, mod, woa, wop, row(post_ln_g[l, 1]), row(post_ln_b[l, 1]),
                 ffn2_w_in[l].astype(bf16), ffn2_w_out[l].astype(bf16),
                 row(post_ln_g[l, 2]), row(post_ln_b[l, 2]))
```

```python
import jax
import jax.numpy as jnp
from jax import lax
from jax.experimental import pallas as pl
from jax.experimental.pallas import tpu as pltpu

f32 = jnp.float32
bf16 = jnp.bfloat16

D_MODEL = 1024
D_FF = 2816
N_HEADS = 8
QK_NOPE = 64
QK_ROPE = 32
V_DIM = 64
Q_LORA = 512
KV_LORA = 256
HEAD_PAD = 128
POOL_WINDOWS = (2, 4, 8, 16)
POOL_GROUP = 128
POOL_WIDTH = 512
POOL_HALO = 16
CHUNK = 64
ROPE_THETA = 10000.0
LN_EPS = 1e-5
RMS_EPS = 1e-6
ALPHA = 2.0 ** 0.25
SM_SCALE = (QK_NOPE + QK_ROPE) ** -0.5
LOG2_E = 1.4426950408889634
V_SLAB = 80
IN_PAD = Q_LORA + KV_LORA + HEAD_PAD + POOL_WIDTH

TM_FFN = 512
TM_MIX = 512
TQ = 256
TK = 256
FF_CHUNKS = (1024, 1024, 768)
VMEM_LIMIT = 56 * 1024 * 1024

_NT = (((1,), (1,)), ((), ()))


def _layernorm(x, g, b):
    mu = jnp.mean(x, axis=-1, keepdims=True)
    xc = x - mu
    var = jnp.mean(xc * xc, axis=-1, keepdims=True)
    return xc * lax.rsqrt(var + LN_EPS) * g + b


def _rmsnorm(x, g):
    ms = jnp.mean(x * x, axis=-1, keepdims=True)
    return x * lax.rsqrt(ms + RMS_EPS) * g


def _swiglu(h, w_in_ref, w_out_ref):
    y = None
    off = 0
    for n in FF_CHUNKS:
        g = jnp.dot(h, w_in_ref[:, off:off + n], preferred_element_type=f32)
        u = jnp.dot(h, w_in_ref[:, D_FF + off:D_FF + off + n], preferred_element_type=f32)
        a = (g / (1.0 + jnp.exp(-g)) * u).astype(bf16)
        part = jnp.dot(a, w_out_ref[off:off + n, :], preferred_element_type=f32)
        y = part if y is None else y + part
        off += n
    return y


def _ada_kernel(c_ref, w_ref, b_ref, o_ref):
    c = c_ref[...]
    c_act = c / (1.0 + jnp.exp(-c))
    o_ref[...] = jnp.dot(c_act, w_ref[...], preferred_element_type=f32,
                         precision=lax.Precision.HIGHEST) + b_ref[...]


def _ada(c, w_ada, b_ada):
    B = c.shape[0]
    n_out = w_ada.shape[1]
    bn = D_MODEL
    return pl.pallas_call(
        _ada_kernel,
        out_shape=jax.ShapeDtypeStruct((B, n_out), f32),
        grid=(n_out // bn,),
        in_specs=[pl.BlockSpec((B, D_MODEL), lambda j: (0, 0)),
                  pl.BlockSpec((D_MODEL, bn), lambda j: (0, j)),
                  pl.BlockSpec((1, bn), lambda j: (0, j))],
        out_specs=pl.BlockSpec((B, bn), lambda j: (0, j)),
        name="ada",
    )(c, w_ada, b_ada.reshape(1, n_out))


def _rope_kernel(pos_ref, f_ref, cos_ref, sin_ref):
    ang = pos_ref[...].astype(f32) * f_ref[...]
    cos_ref[...] = jnp.cos(ang)
    sin_ref[...] = jnp.sin(ang)


def _rope_tables(pos_rep, freq_row):
    rows = pos_rep.shape[0]
    br = 512
    spec = pl.BlockSpec((br, 128), lambda i: (i, 0))
    return pl.pallas_call(
        _rope_kernel,
        out_shape=(jax.ShapeDtypeStruct((rows, 128), f32),) * 2,
        grid=(rows // br,),
        in_specs=[spec, pl.BlockSpec((1, 128), lambda i: (0, 0))],
        out_specs=(spec, spec),
        name="rope_tables",
    )(pos_rep, freq_row)


def _ffn1_kernel(x_ref, mod_ref, lng_ref, lnb_ref, w_in_ref, w_out_ref, pg_ref, pb_ref, o_ref):
    x = _layernorm(x_ref[0], lng_ref[...], lnb_ref[...])
    shift, scale, gate = mod_ref[0, 0:1, :], mod_ref[0, 1:2, :], mod_ref[0, 2:3, :]
    h = (x * (1.0 + scale) + shift).astype(bf16)
    y = _swiglu(h, w_in_ref, w_out_ref)
    o_ref[0] = _layernorm(ALPHA * x + 0.5 * gate * y, pg_ref[...], pb_ref[...])


def _ffn2_kernel(x_ref, attn_ref, yp_ref, mod_ref, woa_ref, wop_ref, pg1_ref, pb1_ref,
                 w_in_ref, w_out_ref, pg2_ref, pb2_ref, o_ref):
    gate1 = mod_ref[0, 5:6, :]
    y = (jnp.dot(attn_ref[0], woa_ref[...], preferred_element_type=f32)
         + jnp.dot(yp_ref[0], wop_ref[...], preferred_element_type=f32))
    x = _layernorm(ALPHA * x_ref[0] + gate1 * y, pg1_ref[...], pb1_ref[...])
    shift, scale, gate = mod_ref[0, 6:7, :], mod_ref[0, 7:8, :], mod_ref[0, 8:9, :]
    h = (x * (1.0 + scale) + shift).astype(bf16)
    y = _swiglu(h, w_in_ref, w_out_ref)
    o_ref[0] = _layernorm(ALPHA * x + 0.5 * gate * y, pg2_ref[...], pb2_ref[...])


def _resident(shape):
    return pl.BlockSpec(shape, lambda b, i: (0,) * len(shape), pipeline_mode=pl.Buffered(1))


def _row_spec(tm, width):
    return pl.BlockSpec((1, tm, width), lambda b, i: (b, i, 0))


def _mod_spec():
    return pl.BlockSpec((1, 9, D_MODEL), lambda b, i: (b, 0, 0))


def _ffn1(x, mod, lng, lnb, w_in, w_out, pg, pb):
    B, S, D = x.shape
    vec = _resident((1, D))
    return pl.pallas_call(
        _ffn1_kernel,
        out_shape=jax.ShapeDtypeStruct((B, S, D), f32),
        grid=(B, S // TM_FFN),
        in_specs=[_row_spec(TM_FFN, D), _mod_spec(), vec, vec,
                  _resident(w_in.shape), _resident(w_out.shape), vec, vec],
        out_specs=_row_spec(TM_FFN, D),
        compiler_params=pltpu.CompilerParams(
            dimension_semantics=("parallel", "parallel"), vmem_limit_bytes=VMEM_LIMIT),
        name="ffn1",
    )(x, mod, lng, lnb, w_in, w_out, pg, pb)


def _ffn2(x1, attn, yp, mod, woa, wop, pg1, pb1, w_in, w_out, pg2, pb2):
    B, S, D = x1.shape
    vec = _resident((1, D))
    return pl.pallas_call(
        _ffn2_kernel,
        out_shape=jax.ShapeDtypeStruct((B, S, D), f32),
        grid=(B, S // TM_FFN),
        in_specs=[_row_spec(TM_FFN, D), _row_spec(TM_FFN, attn.shape[-1]),
                  _row_spec(TM_FFN, yp.shape[-1]), _mod_spec(),
                  _resident(woa.shape), _resident(wop.shape), vec, vec,
                  _resident(w_in.shape), _resident(w_out.shape), vec, vec],
        out_specs=_row_spec(TM_FFN, D),
        compiler_params=pltpu.CompilerParams(
            dimension_semantics=("parallel", "parallel"), vmem_limit_bytes=VMEM_LIMIT),
        name="ffn2",
    )(x1, attn, yp, mod, woa, wop, pg1, pb1, w_in, w_out, pg2, pb2)


def _apply_rope(t, cos_t, sin_t):
    return t * cos_t + pltpu.roll(t, HEAD_PAD - QK_ROPE, 1) * sin_t


def _mix_kernel(x_ref, mod_ref, cos_ref, sin_ref, w_in_ref, qg_ref, kvg_ref, wuq_ref, wuk_ref,
                wuvt_ref, wpool_ref, pbias_ref, pscale_ref,
                q_ref, k_ref, vt_ref, yp_ref, tail_ref):
    st = pl.program_id(1)
    tm = x_ref.shape[1]
    shift, scale = mod_ref[0, 3:4, :], mod_ref[0, 4:5, :]
    h = (x_ref[0] * (1.0 + scale) + shift).astype(bf16)
    proj = jnp.dot(h, w_in_ref[...], preferred_element_type=f32)
    cq = proj[:, :Q_LORA]
    ckv = proj[:, Q_LORA:Q_LORA + KV_LORA]
    krb = proj[:, Q_LORA + KV_LORA:Q_LORA + KV_LORA + HEAD_PAD]
    u = proj[:, Q_LORA + KV_LORA + HEAD_PAD:]

    cos_t, sin_t = cos_ref[0], sin_ref[0]

    cqn = _rmsnorm(cq, qg_ref[...]).astype(bf16)
    q_all = jnp.dot(cqn, wuq_ref[...], preferred_element_type=f32)
    ckvn = _rmsnorm(ckv, kvg_ref[...]).astype(bf16)
    k_all = jnp.dot(ckvn, wuk_ref[...], preferred_element_type=f32)
    v_t = lax.dot_general(wuvt_ref[...], ckvn, _NT, preferred_element_type=f32).astype(bf16)
    ones_rows = (lax.broadcasted_iota(jnp.int32, (V_SLAB - V_DIM, TK), 0) == 0).astype(bf16)
    for t in range(tm // TK):
        for hh in range(N_HEADS):
            vt_ref[0, t, hh * V_SLAB:hh * V_SLAB + V_DIM, :] = (
                v_t[hh * V_DIM:(hh + 1) * V_DIM, t * TK:(t + 1) * TK])
            vt_ref[0, t, hh * V_SLAB + V_DIM:(hh + 1) * V_SLAB, :] = ones_rows
    k_rope = _apply_rope(krb, cos_t, sin_t)
    for hh in range(N_HEADS):
        sl = slice(hh * HEAD_PAD, (hh + 1) * HEAD_PAD)
        q_ref[0, :, sl] = (_apply_rope(q_all[:, sl], cos_t, sin_t) * (SM_SCALE * LOG2_E)).astype(bf16)
        k_ref[0, :, sl] = (k_all[:, sl] + k_rope).astype(bf16)

    @pl.when(st == 0)
    def _():
        tail_ref[...] = jnp.zeros_like(tail_ref)

    ext = jnp.concatenate([tail_ref[...], u], axis=0)
    tail_ref[...] = u[tm - POOL_HALO:, :]
    t_idx = st * tm + lax.broadcasted_iota(jnp.int32, (tm, 1), 0)
    groups = []
    for gi, w in enumerate(POOL_WINDOWS):
        e = ext[:, gi * POOL_GROUP:(gi + 1) * POOL_GROUP]
        sh = 1
        while sh < w:
            e = e + pltpu.roll(e, sh, 0)
            sh *= 2
        cnt = jnp.minimum(t_idx + 1, w).astype(f32)
        groups.append(e[POOL_HALO:, :] / cnt - u[:, gi * POOL_GROUP:(gi + 1) * POOL_GROUP])
    pooled = jnp.concatenate(groups, axis=1).astype(bf16)
    yp = jnp.dot(pooled, wpool_ref[...], preferred_element_type=f32) + pbias_ref[...]
    yp_ref[0] = (yp * pscale_ref[...]).astype(bf16)


def _mix(x1, mod, cos_t, sin_t, w_in, qg, kvg, wuq, wuk, wuvt, wpool, pbias, pscale):
    B, S, D = x1.shape
    tm = TM_MIX
    hw = N_HEADS * HEAD_PAD
    vw = N_HEADS * V_SLAB

    def const(shape):
        return pl.BlockSpec(shape, lambda b, i: (0,) * len(shape))

    return pl.pallas_call(
        _mix_kernel,
        out_shape=(jax.ShapeDtypeStruct((B, S, hw), bf16),
                   jax.ShapeDtypeStruct((B, S, hw), bf16),
                   jax.ShapeDtypeStruct((B, S // TK, vw, TK), bf16),
                   jax.ShapeDtypeStruct((B, S, POOL_WIDTH), bf16)),
        grid=(B, S // tm),
        in_specs=[_row_spec(tm, D), _mod_spec(), _row_spec(tm, HEAD_PAD), _row_spec(tm, HEAD_PAD),
                  const(w_in.shape), const(qg.shape), const(kvg.shape), const(wuq.shape),
                  const(wuk.shape), const(wuvt.shape), const(wpool.shape), const(pbias.shape),
                  const(pscale.shape)],
        out_specs=(_row_spec(tm, hw), _row_spec(tm, hw),
                   pl.BlockSpec((1, tm // TK, vw, TK), lambda b, i: (b, i, 0, 0)),
                   _row_spec(tm, POOL_WIDTH)),
        scratch_shapes=[pltpu.VMEM((POOL_HALO, POOL_WIDTH), f32)],
        compiler_params=pltpu.CompilerParams(
            dimension_semantics=("parallel", "arbitrary"), vmem_limit_bytes=VMEM_LIMIT),
        name="mix_proj",
    )(x1, mod, cos_t, sin_t, w_in, qg, kvg, wuq, wuk, wuvt, wpool, pbias, pscale)


def _attn_kernel(q_ref, k_ref, vt_ref, o_ref, m_ref, acc_ref, s_a, s_b, mt_a, mt_b):
    i = pl.program_id(1)
    m_ref[...] = jnp.full_like(m_ref, -jnp.inf)
    acc_ref[...] = jnp.zeros_like(acc_ref)

    def qk_head(hh, j, s_buf, mt_buf, masked):
        koff = pl.multiple_of(j * TK, TK)
        sl = slice(hh * HEAD_PAD, (hh + 1) * HEAD_PAD)
        s = lax.dot_general(k_ref[0, pl.ds(koff, TK), sl], q_ref[0, :, sl], _NT,
                            preferred_element_type=f32)
        if masked:
            key_chunk = lax.broadcasted_iota(jnp.int32, (TK, TQ), 0) // CHUNK
            qry_chunk = lax.broadcasted_iota(jnp.int32, (TK, TQ), 1) // CHUNK
            s = jnp.where(key_chunk <= qry_chunk, s, -jnp.inf)
        s_buf[hh] = s
        mt_buf[hh] = jnp.max(s, axis=0, keepdims=True)

    def pv_head(hh, j, s_buf, mt_buf):
        m_prev = m_ref[hh]
        m_new = jnp.maximum(m_prev, mt_buf[hh])
        alpha = jnp.exp2(m_prev - m_new)
        p = jnp.exp2(s_buf[hh] - m_new).astype(bf16)
        v_t = vt_ref[0, j, hh * V_SLAB:(hh + 1) * V_SLAB, :]
        acc_ref[hh] = alpha * acc_ref[hh] + jnp.dot(v_t, p, preferred_element_type=f32)
        m_ref[hh] = m_new

    def stage(qk=None, pv=None):
        for hh in range(N_HEADS):
            if qk is not None:
                qk_head(hh, *qk)
            if pv is not None:
                pv_head(hh, *pv)

    stage(qk=(i, s_a, mt_a, True))
    n_pairs = i // 2

    def body(u, carry):
        stage(qk=(2 * u, s_b, mt_b, False), pv=(jnp.where(u == 0, i, 2 * u - 1), s_a, mt_a))
        stage(qk=(2 * u + 1, s_a, mt_a, False), pv=(2 * u, s_b, mt_b))
        return carry

    lax.fori_loop(0, n_pairs, body, 0)
    tile_a = jnp.where(n_pairs == 0, i, 2 * n_pairs - 1)

    @pl.when(i % 2 == 1)
    def _():
        stage(qk=(i - 1, s_b, mt_b, False), pv=(tile_a, s_a, mt_a))
        stage(pv=(i - 1, s_b, mt_b))

    @pl.when(i % 2 == 0)
    def _():
        stage(pv=(tile_a, s_a, mt_a))

    o_t = jnp.concatenate([acc_ref[hh, :V_DIM, :] / acc_ref[hh, V_DIM:V_DIM + 1, :]
                           for hh in range(N_HEADS)], axis=0)
    o_ref[0] = o_t.T.astype(bf16)


def _attention(q, k, v_t):
    B, S, hw = q.shape
    vw = N_HEADS * V_DIM
    return pl.pallas_call(
        _attn_kernel,
        out_shape=jax.ShapeDtypeStruct((B, S, vw), bf16),
        grid=(B, S // TQ),
        in_specs=[_row_spec(TQ, hw),
                  pl.BlockSpec((1, S, hw), lambda b, i: (b, 0, 0)),
                  pl.BlockSpec((1,) + v_t.shape[1:], lambda b, i: (b, 0, 0, 0))],
        out_specs=_row_spec(TQ, vw),
        scratch_shapes=[pltpu.VMEM((N_HEADS, 1, TQ), f32),
                        pltpu.VMEM((N_HEADS, V_SLAB, TQ), f32),
                        pltpu.VMEM((N_HEADS, TK, TQ), f32),
                        pltpu.VMEM((N_HEADS, TK, TQ), f32),
                        pltpu.VMEM((N_HEADS, 1, TQ), f32),
                        pltpu.VMEM((N_HEADS, 1, TQ), f32)],
        compiler_params=pltpu.CompilerParams(
            dimension_semantics=("parallel", "parallel"), vmem_limit_bytes=VMEM_LIMIT),
        name="attention",
    )(q, k, v_t)


def _rot_cols(w):
    half = w.shape[-1] // 2
    return jnp.concatenate([-w[..., half:], w[..., :half]], axis=-1)


def _head_slots(parts):
    k_dim = parts[0][0].shape[0]
    cols = []
    for blocks in parts:
        used = sum(b.shape[1] for b in blocks)
        cols.extend(blocks)
        if used < HEAD_PAD:
            cols.append(jnp.zeros((k_dim, HEAD_PAD - used), f32))
    return jnp.concatenate(cols, axis=1)


def kernel(x, c, positions, ln_in_g, ln_in_b, w_ada, b_ada, ffn1_w_in, ffn1_w_out, w_in, q_norm_g, kv_norm_g, w_uq, w_ukv, pool_w, pool_b, pool_scale, w_out, ffn2_w_in, ffn2_w_out, post_ln_g, post_ln_b):
    B, S, D = x.shape
    l = 0
    row = lambda a: a.reshape(1, -1)

    qh = QK_NOPE + QK_ROPE
    wuq = w_uq[l]
    wuq_pad = _head_slots([[wuq[:, h * qh:h * qh + QK_NOPE],
                            wuq[:, h * qh + QK_NOPE:(h + 1) * qh],
                            _rot_cols(wuq[:, h * qh + QK_NOPE:(h + 1) * qh])]
                           for h in range(N_HEADS)]).astype(bf16)
    kvh = QK_NOPE + V_DIM
    wukv = w_ukv[l]
    wuk_pad = _head_slots([[wukv[:, h * kvh:h * kvh + QK_NOPE]] for h in range(N_HEADS)]).astype(bf16)
    wuv_t = jnp.concatenate([wukv[:, h * kvh + QK_NOPE:(h + 1) * kvh] for h in range(N_HEADS)],
                            axis=1).T.astype(bf16)
    w_kr = w_in[l][:, Q_LORA + KV_LORA:Q_LORA + KV_LORA + QK_ROPE]
    w_in_pad = jnp.concatenate(
        [w_in[l][:, :Q_LORA + KV_LORA], jnp.zeros((D, QK_NOPE), f32), w_kr, _rot_cols(w_kr),
         w_in[l][:, Q_LORA + KV_LORA + QK_ROPE:]], axis=1).astype(bf16)
    wpool = jax.scipy.linalg.block_diag(*[pool_w[l, g] for g in range(len(POOL_WINDOWS))]).astype(bf16)
    wo = w_out[l]
    woa = wo[:N_HEADS * V_DIM].astype(bf16)
    wop = wo[N_HEADS * V_DIM:].astype(bf16)

    inv_freq = ROPE_THETA ** (-jnp.arange(0, QK_ROPE, 2, dtype=f32) / QK_ROPE)
    pos_rep = jnp.repeat(positions.reshape(B * S // 8, 8), QK_ROPE // 2, axis=1)
    cos_c, sin_c = _rope_tables(pos_rep, jnp.tile(inv_freq, 8).reshape(1, 128))
    cos_c = cos_c.reshape(B, S, QK_ROPE // 2)
    sin_c = sin_c.reshape(B, S, QK_ROPE // 2)
    cos_t = jnp.concatenate([jnp.ones((B, S, QK_NOPE), f32), cos_c, cos_c,
                             jnp.zeros((B, S, QK_ROPE), f32)], axis=-1)
    sin_t = jnp.concatenate([jnp.zeros((B, S, QK_NOPE), f32), sin_c, sin_c,
                             jnp.zeros((B, S, QK_ROPE), f32)], axis=-1)

    mod = _ada(c, w_ada[l], b_ada[l]).reshape(B, 9, D)

    x1 = _ffn1(x, mod, row(ln_in_g), row(ln_in_b), ffn1_w_in[l].astype(bf16),
               ffn1_w_out[l].astype(bf16), row(post_ln_g[l, 0]), row(post_ln_b[l, 0]))
    q, k, v_t, yp = _mix(x1, mod, cos_t, sin_t, w_in_pad, row(q_norm_g[l]), row(kv_norm_g[l]),
                         wuq_pad, wuk_pad, wuv_t, wpool, row(pool_b[l]), row(pool_scale[l]))
    attn = _attention(q, k, v_t)
    return _ffn2(x1, attn, yp, mod, woa, wop, row(post_ln_g[l, 1]), row(post_ln_b[l, 1]),
                 ffn2_w_in[l].astype(bf16), ffn2_w_out[l].astype(bf16),
                 row(post_ln_g[l, 2]), row(post_ln_b[l, 2]))
```

```python
import jax
import jax.numpy as jnp
from jax import lax
from jax.experimental import pallas as pl
from jax.experimental.pallas import tpu as pltpu

f32 = jnp.float32
bf16 = jnp.bfloat16

D_MODEL = 1024
D_FF = 2816
N_HEADS = 8
QK_NOPE = 64
QK_ROPE = 32
V_DIM = 64
Q_LORA = 512
KV_LORA = 256
HEAD_PAD = 128
POOL_WINDOWS = (2, 4, 8, 16)
POOL_GROUP = 128
POOL_WIDTH = 512
POOL_HALO = 16
CHUNK = 64
ROPE_THETA = 10000.0
LN_EPS = 1e-5
RMS_EPS = 1e-6
ALPHA = 2.0 ** 0.25
SM_SCALE = (QK_NOPE + QK_ROPE) ** -0.5
LOG2_E = 1.4426950408889634
V_SLAB = 80
IN_PAD = Q_LORA + KV_LORA + HEAD_PAD + POOL_WIDTH

TM_FFN = 1024
TM_SUB = 256
TM_MIX = 512
TQ = 256
TK = 256
FF_CHUNKS = (1024, 1024, 768)
VMEM_LIMIT = 56 * 1024 * 1024

_NT = (((1,), (1,)), ((), ()))


def _layernorm(x, g, b):
    mu = jnp.mean(x, axis=-1, keepdims=True)
    xc = x - mu
    var = jnp.mean(xc * xc, axis=-1, keepdims=True)
    return xc * lax.rsqrt(var + LN_EPS) * g + b


def _rmsnorm(x, g):
    ms = jnp.mean(x * x, axis=-1, keepdims=True)
    return x * lax.rsqrt(ms + RMS_EPS) * g


def _swiglu(h, w_in_ref, w_out_ref):
    y = None
    off = 0
    for n in FF_CHUNKS:
        g = jnp.dot(h, w_in_ref[:, off:off + n], preferred_element_type=f32)
        u = jnp.dot(h, w_in_ref[:, D_FF + off:D_FF + off + n], preferred_element_type=f32)
        a = (g / (1.0 + jnp.exp(-g)) * u).astype(bf16)
        part = jnp.dot(a, w_out_ref[off:off + n, :], preferred_element_type=f32)
        y = part if y is None else y + part
        off += n
    return y


def _ada_kernel(c_ref, w_ref, b_ref, o_ref):
    c = c_ref[...]
    c_act = c / (1.0 + jnp.exp(-c))
    o_ref[...] = jnp.dot(c_act, w_ref[...], preferred_element_type=f32,
                         precision=lax.Precision.HIGHEST) + b_ref[...]


def _ada(c, w_ada, b_ada):
    B = c.shape[0]
    n_out = w_ada.shape[1]
    bn = D_MODEL
    return pl.pallas_call(
        _ada_kernel,
        out_shape=jax.ShapeDtypeStruct((B, n_out), f32),
        grid=(n_out // bn,),
        in_specs=[pl.BlockSpec((B, D_MODEL), lambda j: (0, 0)),
                  pl.BlockSpec((D_MODEL, bn), lambda j: (0, j)),
                  pl.BlockSpec((1, bn), lambda j: (0, j))],
        out_specs=pl.BlockSpec((B, bn), lambda j: (0, j)),
        name="ada",
    )(c, w_ada, b_ada.reshape(1, n_out))


def _rope_kernel(pos_ref, f_ref, cos_ref, sin_ref):
    ang = pos_ref[...].astype(f32) * f_ref[...]
    cos_ref[...] = jnp.cos(ang)
    sin_ref[...] = jnp.sin(ang)


def _rope_tables(pos_rep, freq_row):
    rows = pos_rep.shape[0]
    br = 512
    spec = pl.BlockSpec((br, 128), lambda i: (i, 0))
    return pl.pallas_call(
        _rope_kernel,
        out_shape=(jax.ShapeDtypeStruct((rows, 128), f32),) * 2,
        grid=(rows // br,),
        in_specs=[spec, pl.BlockSpec((1, 128), lambda i: (0, 0))],
        out_specs=(spec, spec),
        name="rope_tables",
    )(pos_rep, freq_row)


def _ffn_block(x, mod_ref, first_row, w_in_ref, w_out_ref, pg_ref, pb_ref):
    shift, scale, gate = (mod_ref[0, first_row + r:first_row + r + 1, :] for r in range(3))
    h = (x * (1.0 + scale) + shift).astype(bf16)
    y = _swiglu(h, w_in_ref, w_out_ref)
    return _layernorm(ALPHA * x + 0.5 * gate * y, pg_ref[...], pb_ref[...])


def _ffn1_kernel(x_ref, mod_ref, lng_ref, lnb_ref, w_in_ref, w_out_ref, pg_ref, pb_ref, o_ref):
    for r0 in range(0, x_ref.shape[1], TM_SUB):
        rows = slice(r0, r0 + TM_SUB)
        x = _layernorm(x_ref[0, rows, :], lng_ref[...], lnb_ref[...])
        o_ref[0, rows, :] = _ffn_block(x, mod_ref, 0, w_in_ref, w_out_ref, pg_ref, pb_ref)


def _ffn2_kernel(x_ref, attn_ref, yp_ref, mod_ref, woa_ref, wop_ref, pg1_ref, pb1_ref,
                 w_in_ref, w_out_ref, pg2_ref, pb2_ref, o_ref):
    gate1 = mod_ref[0, 5:6, :]
    for r0 in range(0, x_ref.shape[1], TM_SUB):
        rows = slice(r0, r0 + TM_SUB)
        y = (jnp.dot(attn_ref[0, rows, :], woa_ref[...], preferred_element_type=f32)
             + jnp.dot(yp_ref[0, rows, :], wop_ref[...], preferred_element_type=f32))
        x = _layernorm(ALPHA * x_ref[0, rows, :] + gate1 * y, pg1_ref[...], pb1_ref[...])
        o_ref[0, rows, :] = _ffn_block(x, mod_ref, 6, w_in_ref, w_out_ref, pg2_ref, pb2_ref)


def _resident(shape):
    return pl.BlockSpec(shape, lambda b, i: (0,) * len(shape), pipeline_mode=pl.Buffered(1))


def _row_spec(tm, width):
    return pl.BlockSpec((1, tm, width), lambda b, i: (b, i, 0))


def _mod_spec():
    return pl.BlockSpec((1, 9, D_MODEL), lambda b, i: (b, 0, 0))


def _ffn1(x, mod, lng, lnb, w_in, w_out, pg, pb):
    B, S, D = x.shape
    vec = _resident((1, D))
    return pl.pallas_call(
        _ffn1_kernel,
        out_shape=jax.ShapeDtypeStruct((B, S, D), f32),
        grid=(B, S // TM_FFN),
        in_specs=[_row_spec(TM_FFN, D), _mod_spec(), vec, vec,
                  _resident(w_in.shape), _resident(w_out.shape), vec, vec],
        out_specs=_row_spec(TM_FFN, D),
        compiler_params=pltpu.CompilerParams(
            dimension_semantics=("parallel", "parallel"), vmem_limit_bytes=VMEM_LIMIT),
        name="ffn1",
    )(x, mod, lng, lnb, w_in, w_out, pg, pb)


def _ffn2(x1, attn, yp, mod, woa, wop, pg1, pb1, w_in, w_out, pg2, pb2):
    B, S, D = x1.shape
    vec = _resident((1, D))
    return pl.pallas_call(
        _ffn2_kernel,
        out_shape=jax.ShapeDtypeStruct((B, S, D), f32),
        grid=(B, S // TM_FFN),
        in_specs=[_row_spec(TM_FFN, D), _row_spec(TM_FFN, attn.shape[-1]),
                  _row_spec(TM_FFN, yp.shape[-1]), _mod_spec(),
                  _resident(woa.shape), _resident(wop.shape), vec, vec,
                  _resident(w_in.shape), _resident(w_out.shape), vec, vec],
        out_specs=_row_spec(TM_FFN, D),
        compiler_params=pltpu.CompilerParams(
            dimension_semantics=("parallel", "parallel"), vmem_limit_bytes=VMEM_LIMIT),
        name="ffn2",
    )(x1, attn, yp, mod, woa, wop, pg1, pb1, w_in, w_out, pg2, pb2)


def _apply_rope(t, cos_t, sin_t):
    return t * cos_t + pltpu.roll(t, HEAD_PAD - QK_ROPE, 1) * sin_t


def _mix_kernel(x_ref, mod_ref, cos_ref, sin_ref, w_in_ref, qg_ref, kvg_ref, wuq_ref, wuk_ref,
                wuvt_ref, wpool_ref, pbias_ref, pscale_ref,
                q_ref, k_ref, vt_ref, yp_ref, tail_ref):
    st = pl.program_id(1)
    tm = x_ref.shape[1]
    shift, scale = mod_ref[0, 3:4, :], mod_ref[0, 4:5, :]
    h = (x_ref[0] * (1.0 + scale) + shift).astype(bf16)
    proj = jnp.dot(h, w_in_ref[...], preferred_element_type=f32)
    cq = proj[:, :Q_LORA]
    ckv = proj[:, Q_LORA:Q_LORA + KV_LORA]
    krb = proj[:, Q_LORA + KV_LORA:Q_LORA + KV_LORA + HEAD_PAD]
    u = proj[:, Q_LORA + KV_LORA + HEAD_PAD:]

    cos_t, sin_t = cos_ref[0], sin_ref[0]

    cqn = _rmsnorm(cq, qg_ref[...]).astype(bf16)
    q_all = jnp.dot(cqn, wuq_ref[...], preferred_element_type=f32)
    ckvn = _rmsnorm(ckv, kvg_ref[...]).astype(bf16)
    k_all = jnp.dot(ckvn, wuk_ref[...], preferred_element_type=f32)
    v_t = lax.dot_general(wuvt_ref[...], ckvn, _NT, preferred_element_type=f32).astype(bf16)
    ones_rows = (lax.broadcasted_iota(jnp.int32, (V_SLAB - V_DIM, TK), 0) == 0).astype(bf16)
    for t in range(tm // TK):
        for hh in range(N_HEADS):
            vt_ref[0, t, hh * V_SLAB:hh * V_SLAB + V_DIM, :] = (
                v_t[hh * V_DIM:(hh + 1) * V_DIM, t * TK:(t + 1) * TK])
            vt_ref[0, t, hh * V_SLAB + V_DIM:(hh + 1) * V_SLAB, :] = ones_rows
    k_rope = _apply_rope(krb, cos_t, sin_t)
    for hh in range(N_HEADS):
        sl = slice(hh * HEAD_PAD, (hh + 1) * HEAD_PAD)
        q_ref[0, :, sl] = (_apply_rope(q_all[:, sl], cos_t, sin_t) * (SM_SCALE * LOG2_E)).astype(bf16)
        k_ref[0, :, sl] = (k_all[:, sl] + k_rope).astype(bf16)

    @pl.when(st == 0)
    def _():
        tail_ref[...] = jnp.zeros_like(tail_ref)

    ext = jnp.concatenate([tail_ref[...], u], axis=0)
    tail_ref[...] = u[tm - POOL_HALO:, :]
    t_idx = st * tm + lax.broadcasted_iota(jnp.int32, (tm, 1), 0)
    groups = []
    for gi, w in enumerate(POOL_WINDOWS):
        e = ext[:, gi * POOL_GROUP:(gi + 1) * POOL_GROUP]
        sh = 1
        while sh < w:
            e = e + pltpu.roll(e, sh, 0)
            sh *= 2
        cnt = jnp.minimum(t_idx + 1, w).astype(f32)
        groups.append(e[POOL_HALO:, :] / cnt - u[:, gi * POOL_GROUP:(gi + 1) * POOL_GROUP])
    pooled = jnp.concatenate(groups, axis=1).astype(bf16)
    yp = jnp.dot(pooled, wpool_ref[...], preferred_element_type=f32) + pbias_ref[...]
    yp_ref[0] = (yp * pscale_ref[...]).astype(bf16)


def _mix(x1, mod, cos_t, sin_t, w_in, qg, kvg, wuq, wuk, wuvt, wpool, pbias, pscale):
    B, S, D = x1.shape
    tm = TM_MIX
    hw = N_HEADS * HEAD_PAD
    vw = N_HEADS * V_SLAB

    def const(shape):
        return pl.BlockSpec(shape, lambda b, i: (0,) * len(shape))

    return pl.pallas_call(
        _mix_kernel,
        out_shape=(jax.ShapeDtypeStruct((B, S, hw), bf16),
                   jax.ShapeDtypeStruct((B, S, hw), bf16),
                   jax.ShapeDtypeStruct((B, S // TK, vw, TK), bf16),
                   jax.ShapeDtypeStruct((B, S, POOL_WIDTH), bf16)),
        grid=(B, S // tm),
        in_specs=[_row_spec(tm, D), _mod_spec(), _row_spec(tm, HEAD_PAD), _row_spec(tm, HEAD_PAD),
                  const(w_in.shape), const(qg.shape), const(kvg.shape), const(wuq.shape),
                  const(wuk.shape), const(wuvt.shape), const(wpool.shape), const(pbias.shape),
                  const(pscale.shape)],
        out_specs=(_row_spec(tm, hw), _row_spec(tm, hw),
                   pl.BlockSpec((1, tm // TK, vw, TK), lambda b, i: (b, i, 0, 0)),
                   _row_spec(tm, POOL_WIDTH)),
        scratch_shapes=[pltpu.VMEM((POOL_HALO, POOL_WIDTH), f32)],
        compiler_params=pltpu.CompilerParams(
            dimension_semantics=("parallel", "arbitrary"), vmem_limit_bytes=VMEM_LIMIT),
        name="mix_proj",
    )(x1, mod, cos_t, sin_t, w_in, qg, kvg, wuq, wuk, wuvt, wpool, pbias, pscale)


def _attn_kernel(q_ref, k_ref, vt_ref, o_ref, m_ref, acc_ref, s_a, s_b, mt_a, mt_b):
    i = pl.program_id(1)
    m_ref[...] = jnp.full_like(m_ref, -jnp.inf)
    acc_ref[...] = jnp.zeros_like(acc_ref)

    def qk_head(hh, j, s_buf, mt_buf, masked):
        koff = pl.multiple_of(j * TK, TK)
        sl = slice(hh * HEAD_PAD, (hh + 1) * HEAD_PAD)
        s = lax.dot_general(k_ref[0, pl.ds(koff, TK), sl], q_ref[0, :, sl], _NT,
                            preferred_element_type=f32)
        if masked:
            key_chunk = lax.broadcasted_iota(jnp.int32, (TK, TQ), 0) // CHUNK
            qry_chunk = lax.broadcasted_iota(jnp.int32, (TK, TQ), 1) // CHUNK
            s = jnp.where(key_chunk <= qry_chunk, s, -jnp.inf)
        s_buf[hh] = s
        mt_buf[hh] = jnp.max(s, axis=0, keepdims=True)

    def pv_head(hh, j, s_buf, mt_buf):
        m_prev = m_ref[hh]
        m_new = jnp.maximum(m_prev, mt_buf[hh])
        alpha = jnp.exp2(m_prev - m_new)
        p = jnp.exp2(s_buf[hh] - m_new).astype(bf16)
        v_t = vt_ref[0, j, hh * V_SLAB:(hh + 1) * V_SLAB, :]
        acc_ref[hh] = alpha * acc_ref[hh] + jnp.dot(v_t, p, preferred_element_type=f32)
        m_ref[hh] = m_new

    def stage(qk=None, pv=None):
        for hh in range(N_HEADS):
            if qk is not None:
                qk_head(hh, *qk)
            if pv is not None:
                pv_head(hh, *pv)

    stage(qk=(i, s_a, mt_a, True))
    n_pairs = i // 2

    def body(u, carry):
        stage(qk=(2 * u, s_b, mt_b, False), pv=(jnp.where(u == 0, i, 2 * u - 1), s_a, mt_a))
        stage(qk=(2 * u + 1, s_a, mt_a, False), pv=(2 * u, s_b, mt_b))
        return carry

    lax.fori_loop(0, n_pairs, body, 0)
    tile_a = jnp.where(n_pairs == 0, i, 2 * n_pairs - 1)

    @pl.when(i % 2 == 1)
    def _():
        stage(qk=(i - 1, s_b, mt_b, False), pv=(tile_a, s_a, mt_a))
        stage(pv=(i - 1, s_b, mt_b))

    @pl.when(i % 2 == 0)
    def _():
        stage(pv=(tile_a, s_a, mt_a))

    o_t = jnp.concatenate([acc_ref[hh, :V_DIM, :] / acc_ref[hh, V_DIM:V_DIM + 1, :]
                           for hh in range(N_HEADS)], axis=0)
    o_ref[0] = o_t.T.astype(bf16)


def _attention(q, k, v_t):
    B, S, hw = q.shape
    vw = N_HEADS * V_DIM
    return pl.pallas_call(
        _attn_kernel,
        out_shape=jax.ShapeDtypeStruct((B, S, vw), bf16),
        grid=(B, S // TQ),
        in_specs=[_row_spec(TQ, hw),
                  pl.BlockSpec((1, S, hw), lambda b, i: (b, 0, 0)),
                  pl.BlockSpec((1,) + v_t.shape[1:], lambda b, i: (b, 0, 0, 0))],
        out_specs=_row_spec(TQ, vw),
        scratch_shapes=[pltpu.VMEM((N_HEADS, 1, TQ), f32),
                        pltpu.VMEM((N_HEADS, V_SLAB, TQ), f32),
                        pltpu.VMEM((N_HEADS, TK, TQ), f32),
                        pltpu.VMEM((N_HEADS, TK, TQ), f32),
                        pltpu.VMEM((N_HEADS, 1, TQ), f32),
                        pltpu.VMEM((N_HEADS, 1, TQ), f32)],
        compiler_params=pltpu.CompilerParams(
            dimension_semantics=("parallel", "parallel"), vmem_limit_bytes=VMEM_LIMIT),
        name="attention",
    )(q, k, v_t)


def _rot_cols(w):
    half = w.shape[-1] // 2
    return jnp.concatenate([-w[..., half:], w[..., :half]], axis=-1)


def _head_slots(parts):
    k_dim = parts[0][0].shape[0]
    cols = []
    for blocks in parts:
        used = sum(b.shape[1] for b in blocks)
        cols.extend(blocks)
        if used < HEAD_PAD:
            cols.append(jnp.zeros((k_dim, HEAD_PAD - used), f32))
    return jnp.concatenate(cols, axis=1)


def kernel(x, c, positions, ln_in_g, ln_in_b, w_ada, b_ada, ffn1_w_in, ffn1_w_out, w_in, q_norm_g, kv_norm_g, w_uq, w_ukv, pool_w, pool_b, pool_scale, w_out, ffn2_w_in, ffn2_w_out, post_ln_g, post_ln_b):
    B, S, D = x.shape
    l = 0
    row = lambda a: a.reshape(1, -1)

    qh = QK_NOPE + QK_ROPE
    wuq = w_uq[l]
    wuq_pad = _head_slots([[wuq[:, h * qh:h * qh + QK_NOPE],
                            wuq[:, h * qh + QK_NOPE:(h + 1) * qh],
                            _rot_cols(wuq[:, h * qh + QK_NOPE:(h + 1) * qh])]
                           for h in range(N_HEADS)]).astype(bf16)
    kvh = QK_NOPE + V_DIM
    wukv = w_ukv[l]
    wuk_pad = _head_slots([[wukv[:, h * kvh:h * kvh + QK_NOPE]] for h in range(N_HEADS)]).astype(bf16)
    wuv_t = jnp.concatenate([wukv[:, h * kvh + QK_NOPE:(h + 1) * kvh] for h in range(N_HEADS)],
                            axis=1).T.astype(bf16)
    w_kr = w_in[l][:, Q_LORA + KV_LORA:Q_LORA + KV_LORA + QK_ROPE]
    w_in_pad = jnp.concatenate(
        [w_in[l][:, :Q_LORA + KV_LORA], jnp.zeros((D, QK_NOPE), f32), w_kr, _rot_cols(w_kr),
         w_in[l][:, Q_LORA + KV_LORA + QK_ROPE:]], axis=1).astype(bf16)
    wpool = jax.scipy.linalg.block_diag(*[pool_w[l, g] for g in range(len(POOL_WINDOWS))]).astype(bf16)
    wo = w_out[l]
    woa = wo[:N_HEADS * V_DIM].astype(bf16)
    wop = wo[N_HEADS * V_DIM:].astype(bf16)

    inv_freq = ROPE_THETA ** (-jnp.arange(0, QK_ROPE, 2, dtype=f32) / QK_ROPE)
    pos_rep = jnp.repeat(positions.reshape(B * S // 8, 8), QK_ROPE // 2, axis=1)
    cos_c, sin_c = _rope_tables(pos_rep, jnp.tile(inv_freq, 8).reshape(1, 128))
    cos_c = cos_c.reshape(B, S, QK_ROPE // 2)
    sin_c = sin_c.reshape(B, S, QK_ROPE // 2)
    cos_t = jnp.concatenate([jnp.ones((B, S, QK_NOPE), f32), cos_c, cos_c,
                             jnp.zeros((B, S, QK_ROPE), f32)], axis=-1)
    sin_t = jnp.concatenate([jnp.zeros((B, S, QK_NOPE), f32), sin_c, sin_c,
                             jnp.zeros((B, S, QK_ROPE), f32)], axis=-1)

    mod = _ada(c, w_ada[l], b_ada[l]).reshape(B, 9, D)

    x1 = _ffn1(x, mod, row(ln_in_g), row(ln_in_b), ffn1_w_in[l].astype(bf16),
               ffn1_w_out[l].astype(bf16), row(post_ln_g[l, 0]), row(post_ln_b[l, 0]))
    q, k, v_t, yp = _mix(x1, mod, cos_t, sin_t, w_in_pad, row(q_norm_g[l]), row(kv_norm_g[l]),
                         wuq_pad, wuk_pad, wuv_t, wpool, row(pool_b[l]), row(pool_scale[l]))
    attn = _attention(q, k, v_t)
    return _ffn2(x1, attn, yp, mod, woa, wop, row(post_ln_g[l, 1]), row(post_ln_b[l, 1]),
                 ffn2_w_in[l].astype(bf16), ffn2_w_out[l].astype(bf16),
                 row(post_ln_g[l, 2]), row(post_ln_b[l, 2]))
```

```python
import jax
import jax.numpy as jnp
from jax import lax
from jax.experimental import pallas as pl
from jax.experimental.pallas import tpu as pltpu

f32 = jnp.float32
bf16 = jnp.bfloat16

D_MODEL = 1024
D_FF = 2816
N_HEADS = 8
QK_NOPE = 64
QK_ROPE = 32
V_DIM = 64
Q_LORA = 512
KV_LORA = 256
HEAD_PAD = 128
POOL_WINDOWS = (2, 4, 8, 16)
POOL_GROUP = 128
POOL_WIDTH = 512
POOL_HALO = 16
CHUNK = 64
ROPE_THETA = 10000.0
LN_EPS = 1e-5
RMS_EPS = 1e-6
ALPHA = 2.0 ** 0.25
SM_SCALE = (QK_NOPE + QK_ROPE) ** -0.5
LOG2_E = 1.4426950408889634
V_SLAB = 80
IN_PAD = Q_LORA + KV_LORA + HEAD_PAD + POOL_WIDTH

TM_FFN = 1024
TM_SUB = 256
TM_MIX = 1024
TQ = 256
TK = 256
FF_CHUNKS = (1024, 1024, 768)
VMEM_LIMIT = 56 * 1024 * 1024

_NT = (((1,), (1,)), ((), ()))


def _layernorm(x, g, b):
    mu = jnp.mean(x, axis=-1, keepdims=True)
    xc = x - mu
    var = jnp.mean(xc * xc, axis=-1, keepdims=True)
    return xc * lax.rsqrt(var + LN_EPS) * g + b


def _rmsnorm(x, g):
    ms = jnp.mean(x * x, axis=-1, keepdims=True)
    return x * lax.rsqrt(ms + RMS_EPS) * g


def _swiglu(h, w_in_ref, w_out_ref):
    y = None
    off = 0
    for n in FF_CHUNKS:
        g = jnp.dot(h, w_in_ref[:, off:off + n], preferred_element_type=f32)
        u = jnp.dot(h, w_in_ref[:, D_FF + off:D_FF + off + n], preferred_element_type=f32)
        a = (g / (1.0 + jnp.exp(-g)) * u).astype(bf16)
        part = jnp.dot(a, w_out_ref[off:off + n, :], preferred_element_type=f32)
        y = part if y is None else y + part
        off += n
    return y


def _ada_kernel(c_ref, w_ref, b_ref, o_ref):
    c = c_ref[...]
    c_act = c / (1.0 + jnp.exp(-c))
    o_ref[...] = jnp.dot(c_act, w_ref[...], preferred_element_type=f32,
                         precision=lax.Precision.HIGHEST) + b_ref[...]


def _ada(c, w_ada, b_ada):
    B = c.shape[0]
    n_out = w_ada.shape[1]
    bn = D_MODEL
    return pl.pallas_call(
        _ada_kernel,
        out_shape=jax.ShapeDtypeStruct((B, n_out), f32),
        grid=(n_out // bn,),
        in_specs=[pl.BlockSpec((B, D_MODEL), lambda j: (0, 0)),
                  pl.BlockSpec((D_MODEL, bn), lambda j: (0, j)),
                  pl.BlockSpec((1, bn), lambda j: (0, j))],
        out_specs=pl.BlockSpec((B, bn), lambda j: (0, j)),
        name="ada",
    )(c, w_ada, b_ada.reshape(1, n_out))


def _rope_kernel(pos_ref, f_ref, cos_ref, sin_ref):
    ang = pos_ref[...].astype(f32) * f_ref[...]
    cos_ref[...] = jnp.cos(ang)
    sin_ref[...] = jnp.sin(ang)


def _rope_tables(pos_rep, freq_row):
    rows = pos_rep.shape[0]
    br = 512
    spec = pl.BlockSpec((br, 128), lambda i: (i, 0))
    return pl.pallas_call(
        _rope_kernel,
        out_shape=(jax.ShapeDtypeStruct((rows, 128), f32),) * 2,
        grid=(rows // br,),
        in_specs=[spec, pl.BlockSpec((1, 128), lambda i: (0, 0))],
        out_specs=(spec, spec),
        name="rope_tables",
    )(pos_rep, freq_row)


def _ffn_block(x, mod_ref, first_row, w_in_ref, w_out_ref, pg_ref, pb_ref):
    shift, scale, gate = (mod_ref[0, first_row + r:first_row + r + 1, :] for r in range(3))
    h = (x * (1.0 + scale) + shift).astype(bf16)
    y = _swiglu(h, w_in_ref, w_out_ref)
    return _layernorm(ALPHA * x + 0.5 * gate * y, pg_ref[...], pb_ref[...])


def _ffn1_kernel(x_ref, mod_ref, lng_ref, lnb_ref, w_in_ref, w_out_ref, pg_ref, pb_ref, o_ref):
    for r0 in range(0, x_ref.shape[1], TM_SUB):
        rows = slice(r0, r0 + TM_SUB)
        x = _layernorm(x_ref[0, rows, :], lng_ref[...], lnb_ref[...])
        o_ref[0, rows, :] = _ffn_block(x, mod_ref, 0, w_in_ref, w_out_ref, pg_ref, pb_ref)


def _ffn2_kernel(x_ref, attn_ref, yp_ref, mod_ref, woa_ref, wop_ref, pg1_ref, pb1_ref,
                 w_in_ref, w_out_ref, pg2_ref, pb2_ref, o_ref):
    gate1 = mod_ref[0, 5:6, :]
    for r0 in range(0, x_ref.shape[1], TM_SUB):
        rows = slice(r0, r0 + TM_SUB)
        y = (jnp.dot(attn_ref[0, rows, :], woa_ref[...], preferred_element_type=f32)
             + jnp.dot(yp_ref[0, rows, :], wop_ref[...], preferred_element_type=f32))
        x = _layernorm(ALPHA * x_ref[0, rows, :] + gate1 * y, pg1_ref[...], pb1_ref[...])
        o_ref[0, rows, :] = _ffn_block(x, mod_ref, 6, w_in_ref, w_out_ref, pg2_ref, pb2_ref)


def _resident(shape):
    return pl.BlockSpec(shape, lambda b, i: (0,) * len(shape), pipeline_mode=pl.Buffered(1))


def _row_spec(tm, width):
    return pl.BlockSpec((1, tm, width), lambda b, i: (b, i, 0))


def _mod_spec():
    return pl.BlockSpec((1, 9, D_MODEL), lambda b, i: (b, 0, 0))


def _ffn1(x, mod, lng, lnb, w_in, w_out, pg, pb):
    B, S, D = x.shape
    vec = _resident((1, D))
    return pl.pallas_call(
        _ffn1_kernel,
        out_shape=jax.ShapeDtypeStruct((B, S, D), f32),
        grid=(B, S // TM_FFN),
        in_specs=[_row_spec(TM_FFN, D), _mod_spec(), vec, vec,
                  _resident(w_in.shape), _resident(w_out.shape), vec, vec],
        out_specs=_row_spec(TM_FFN, D),
        compiler_params=pltpu.CompilerParams(
            dimension_semantics=("parallel", "parallel"), vmem_limit_bytes=VMEM_LIMIT),
        name="ffn1",
    )(x, mod, lng, lnb, w_in, w_out, pg, pb)


def _ffn2(x1, attn, yp, mod, woa, wop, pg1, pb1, w_in, w_out, pg2, pb2):
    B, S, D = x1.shape
    vec = _resident((1, D))
    return pl.pallas_call(
        _ffn2_kernel,
        out_shape=jax.ShapeDtypeStruct((B, S, D), f32),
        grid=(B, S // TM_FFN),
        in_specs=[_row_spec(TM_FFN, D), _row_spec(TM_FFN, attn.shape[-1]),
                  _row_spec(TM_FFN, yp.shape[-1]), _mod_spec(),
                  _resident(woa.shape), _resident(wop.shape), vec, vec,
                  _resident(w_in.shape), _resident(w_out.shape), vec, vec],
        out_specs=_row_spec(TM_FFN, D),
        compiler_params=pltpu.CompilerParams(
            dimension_semantics=("parallel", "parallel"), vmem_limit_bytes=VMEM_LIMIT),
        name="ffn2",
    )(x1, attn, yp, mod, woa, wop, pg1, pb1, w_in, w_out, pg2, pb2)


def _apply_rope(t, cos_t, sin_t):
    return t * cos_t + pltpu.roll(t, HEAD_PAD - QK_ROPE, 1) * sin_t


def _mix_kernel(x_ref, mod_ref, cos_ref, sin_ref, w_in_ref, qg_ref, kvg_ref, wuq_ref, wuk_ref,
                wuvt_ref, wpool_ref, pbias_ref, pscale_ref,
                q_ref, k_ref, vt_ref, yp_ref, tail_ref):
    st = pl.program_id(1)
    tm = x_ref.shape[1]
    shift, scale = mod_ref[0, 3:4, :], mod_ref[0, 4:5, :]
    ones_rows = (lax.broadcasted_iota(jnp.int32, (V_SLAB - V_DIM, TK), 0) == 0).astype(bf16)

    @pl.when(st == 0)
    def _():
        tail_ref[...] = jnp.zeros_like(tail_ref)

    tail = tail_ref[...]
    for t in range(tm // TK):
        rows = slice(t * TK, (t + 1) * TK)
        h = (x_ref[0, rows, :] * (1.0 + scale) + shift).astype(bf16)
        proj = jnp.dot(h, w_in_ref[...], preferred_element_type=f32)
        cq = proj[:, :Q_LORA]
        ckv = proj[:, Q_LORA:Q_LORA + KV_LORA]
        krb = proj[:, Q_LORA + KV_LORA:Q_LORA + KV_LORA + HEAD_PAD]
        u = proj[:, Q_LORA + KV_LORA + HEAD_PAD:]
        cos_t, sin_t = cos_ref[0, rows, :], sin_ref[0, rows, :]

        cqn = _rmsnorm(cq, qg_ref[...]).astype(bf16)
        q_all = jnp.dot(cqn, wuq_ref[...], preferred_element_type=f32)
        ckvn = _rmsnorm(ckv, kvg_ref[...]).astype(bf16)
        k_all = jnp.dot(ckvn, wuk_ref[...], preferred_element_type=f32)
        v_t = lax.dot_general(wuvt_ref[...], ckvn, _NT, preferred_element_type=f32).astype(bf16)
        for hh in range(N_HEADS):
            vt_ref[0, t, hh * V_SLAB:hh * V_SLAB + V_DIM, :] = v_t[hh * V_DIM:(hh + 1) * V_DIM, :]
            vt_ref[0, t, hh * V_SLAB + V_DIM:(hh + 1) * V_SLAB, :] = ones_rows
        k_rope = _apply_rope(krb, cos_t, sin_t)
        for hh in range(N_HEADS):
            sl = slice(hh * HEAD_PAD, (hh + 1) * HEAD_PAD)
            q_ref[0, rows, sl] = (_apply_rope(q_all[:, sl], cos_t, sin_t)
                                  * (SM_SCALE * LOG2_E)).astype(bf16)
            k_ref[0, rows, sl] = (k_all[:, sl] + k_rope).astype(bf16)

        ext = jnp.concatenate([tail, u], axis=0)
        tail = u[TK - POOL_HALO:, :]
        t_idx = st * tm + t * TK + lax.broadcasted_iota(jnp.int32, (TK, 1), 0)
        groups = []
        for gi, w in enumerate(POOL_WINDOWS):
            e = ext[:, gi * POOL_GROUP:(gi + 1) * POOL_GROUP]
            sh = 1
            while sh < w:
                e = e + pltpu.roll(e, sh, 0)
                sh *= 2
            cnt = jnp.minimum(t_idx + 1, w).astype(f32)
            groups.append(e[POOL_HALO:, :] / cnt - u[:, gi * POOL_GROUP:(gi + 1) * POOL_GROUP])
        pooled = jnp.concatenate(groups, axis=1).astype(bf16)
        yp = jnp.dot(pooled, wpool_ref[...], preferred_element_type=f32) + pbias_ref[...]
        yp_ref[0, rows, :] = (yp * pscale_ref[...]).astype(bf16)
    tail_ref[...] = tail


def _mix(x1, mod, cos_t, sin_t, w_in, qg, kvg, wuq, wuk, wuvt, wpool, pbias, pscale):
    B, S, D = x1.shape
    tm = TM_MIX
    hw = N_HEADS * HEAD_PAD
    vw = N_HEADS * V_SLAB

    def const(shape):
        return pl.BlockSpec(shape, lambda b, i: (0,) * len(shape))

    return pl.pallas_call(
        _mix_kernel,
        out_shape=(jax.ShapeDtypeStruct((B, S, hw), bf16),
                   jax.ShapeDtypeStruct((B, S, hw), bf16),
                   jax.ShapeDtypeStruct((B, S // TK, vw, TK), bf16),
                   jax.ShapeDtypeStruct((B, S, POOL_WIDTH), bf16)),
        grid=(B, S // tm),
        in_specs=[_row_spec(tm, D), _mod_spec(), _row_spec(tm, HEAD_PAD), _row_spec(tm, HEAD_PAD),
                  const(w_in.shape), const(qg.shape), const(kvg.shape), const(wuq.shape),
                  const(wuk.shape), const(wuvt.shape), const(wpool.shape), const(pbias.shape),
                  const(pscale.shape)],
        out_specs=(_row_spec(tm, hw), _row_spec(tm, hw),
                   pl.BlockSpec((1, tm // TK, vw, TK), lambda b, i: (b, i, 0, 0)),
                   _row_spec(tm, POOL_WIDTH)),
        scratch_shapes=[pltpu.VMEM((POOL_HALO, POOL_WIDTH), f32)],
        compiler_params=pltpu.CompilerParams(
            dimension_semantics=("parallel", "arbitrary"), vmem_limit_bytes=VMEM_LIMIT),
        name="mix_proj",
    )(x1, mod, cos_t, sin_t, w_in, qg, kvg, wuq, wuk, wuvt, wpool, pbias, pscale)


def _attn_kernel(q_ref, k_ref, vt_ref, o_ref, m_ref, acc_ref, s_a, s_b, mt_a, mt_b):
    i = pl.program_id(1)
    m_ref[...] = jnp.full_like(m_ref, -jnp.inf)
    acc_ref[...] = jnp.zeros_like(acc_ref)

    def qk_head(hh, j, s_buf, mt_buf, diag):
        koff = pl.multiple_of(j * TK, TK)
        sl = slice(hh * HEAD_PAD, (hh + 1) * HEAD_PAD)
        s = lax.dot_general(k_ref[0, pl.ds(koff, TK), sl], q_ref[0, :, sl], _NT,
                            preferred_element_type=f32)
        if diag is not None:
            key_chunk = lax.broadcasted_iota(jnp.int32, (TK, TQ), 0) // CHUNK + diag * (TK // CHUNK)
            qry_chunk = lax.broadcasted_iota(jnp.int32, (TK, TQ), 1) // CHUNK
            s = jnp.where(key_chunk <= qry_chunk, s, -jnp.inf)
        s_buf[hh] = s
        mt_buf[hh] = jnp.max(s, axis=0, keepdims=True)

    def pv_head(hh, j, s_buf, mt_buf):
        m_prev = m_ref[hh]
        m_new = jnp.maximum(m_prev, mt_buf[hh])
        alpha = jnp.exp2(m_prev - m_new)
        p = jnp.exp2(s_buf[hh] - m_new).astype(bf16)
        v_t = vt_ref[0, j, hh * V_SLAB:(hh + 1) * V_SLAB, :]
        acc_ref[hh] = alpha * acc_ref[hh] + jnp.dot(v_t, p, preferred_element_type=f32)
        m_ref[hh] = m_new

    def stage(qk=None, pv=None):
        for hh in range(N_HEADS):
            if qk is not None:
                qk_head(hh, *qk)
            if pv is not None:
                pv_head(hh, *pv)

    stage(qk=(i, s_a, mt_a, 0))
    n_pairs = i // 2

    def pair(u):
        stage(qk=(2 * u, s_b, mt_b, None), pv=(jnp.where(u == 0, i, 2 * u - 1), s_a, mt_a))
        stage(qk=(2 * u + 1, s_a, mt_a, None), pv=(2 * u, s_b, mt_b))

    def body(v, carry):
        pair(2 * v)
        pair(2 * v + 1)
        return carry

    lax.fori_loop(0, n_pairs // 2, body, 0)

    @pl.when(n_pairs % 2 == 1)
    def _():
        pair(n_pairs - 1)

    tile_a = jnp.where(n_pairs == 0, i, 2 * n_pairs - 1)

    @pl.when(i % 2 == 1)
    def _():
        stage(qk=(i - 1, s_b, mt_b, None), pv=(tile_a, s_a, mt_a))
        stage(pv=(i - 1, s_b, mt_b))

    @pl.when(i % 2 == 0)
    def _():
        stage(pv=(tile_a, s_a, mt_a))

    o_t = jnp.concatenate([acc_ref[hh, :V_DIM, :] / acc_ref[hh, V_DIM:V_DIM + 1, :]
                           for hh in range(N_HEADS)], axis=0)
    o_ref[0] = o_t.T.astype(bf16)


def _attention(q, k, v_t):
    B, S, hw = q.shape
    vw = N_HEADS * V_DIM
    return pl.pallas_call(
        _attn_kernel,
        out_shape=jax.ShapeDtypeStruct((B, S, vw), bf16),
        grid=(B, S // TQ),
        in_specs=[_row_spec(TQ, hw),
                  pl.BlockSpec((1, S, hw), lambda b, i: (b, 0, 0)),
                  pl.BlockSpec((1,) + v_t.shape[1:], lambda b, i: (b, 0, 0, 0))],
        out_specs=_row_spec(TQ, vw),
        scratch_shapes=[pltpu.VMEM((N_HEADS, 1, TQ), f32),
                        pltpu.VMEM((N_HEADS, V_SLAB, TQ), f32),
                        pltpu.VMEM((N_HEADS, TK, TQ), f32),
                        pltpu.VMEM((N_HEADS, TK, TQ), f32),
                        pltpu.VMEM((N_HEADS, 1, TQ), f32),
                        pltpu.VMEM((N_HEADS, 1, TQ), f32)],
        compiler_params=pltpu.CompilerParams(
            dimension_semantics=("parallel", "parallel"), vmem_limit_bytes=VMEM_LIMIT),
        name="attention",
    )(q, k, v_t)


def _rot_cols(w):
    half = w.shape[-1] // 2
    return jnp.concatenate([-w[..., half:], w[..., :half]], axis=-1)


def _head_slots(parts):
    k_dim = parts[0][0].shape[0]
    cols = []
    for blocks in parts:
        used = sum(b.shape[1] for b in blocks)
        cols.extend(blocks)
        if used < HEAD_PAD:
            cols.append(jnp.zeros((k_dim, HEAD_PAD - used), f32))
    return jnp.concatenate(cols, axis=1)


def kernel(x, c, positions, ln_in_g, ln_in_b, w_ada, b_ada, ffn1_w_in, ffn1_w_out, w_in, q_norm_g, kv_norm_g, w_uq, w_ukv, pool_w, pool_b, pool_scale, w_out, ffn2_w_in, ffn2_w_out, post_ln_g, post_ln_b):
    B, S, D = x.shape
    l = 0
    row = lambda a: a.reshape(1, -1)

    qh = QK_NOPE + QK_ROPE
    wuq = w_uq[l]
    wuq_pad = _head_slots([[wuq[:, h * qh:h * qh + QK_NOPE],
                            wuq[:, h * qh + QK_NOPE:(h + 1) * qh],
                            _rot_cols(wuq[:, h * qh + QK_NOPE:(h + 1) * qh])]
                           for h in range(N_HEADS)]).astype(bf16)
    kvh = QK_NOPE + V_DIM
    wukv = w_ukv[l]
    wuk_pad = _head_slots([[wukv[:, h * kvh:h * kvh + QK_NOPE]] for h in range(N_HEADS)]).astype(bf16)
    wuv_t = jnp.concatenate([wukv[:, h * kvh + QK_NOPE:(h + 1) * kvh] for h in range(N_HEADS)],
                            axis=1).T.astype(bf16)
    w_kr = w_in[l][:, Q_LORA + KV_LORA:Q_LORA + KV_LORA + QK_ROPE]
    w_in_pad = jnp.concatenate(
        [w_in[l][:, :Q_LORA + KV_LORA], jnp.zeros((D, QK_NOPE), f32), w_kr, _rot_cols(w_kr),
         w_in[l][:, Q_LORA + KV_LORA + QK_ROPE:]], axis=1).astype(bf16)
    wpool = jax.scipy.linalg.block_diag(*[pool_w[l, g] for g in range(len(POOL_WINDOWS))]).astype(bf16)
    wo = w_out[l]
    woa = wo[:N_HEADS * V_DIM].astype(bf16)
    wop = wo[N_HEADS * V_DIM:].astype(bf16)

    inv_freq = ROPE_THETA ** (-jnp.arange(0, QK_ROPE, 2, dtype=f32) / QK_ROPE)
    pos_rep = jnp.repeat(positions.reshape(B * S // 8, 8), QK_ROPE // 2, axis=1)
    cos_c, sin_c = _rope_tables(pos_rep, jnp.tile(inv_freq, 8).reshape(1, 128))
    cos_c = cos_c.reshape(B, S, QK_ROPE // 2)
    sin_c = sin_c.reshape(B, S, QK_ROPE // 2)
    cos_t = jnp.concatenate([jnp.ones((B, S, QK_NOPE), f32), cos_c, cos_c,
                             jnp.zeros((B, S, QK_ROPE), f32)], axis=-1)
    sin_t = jnp.concatenate([jnp.zeros((B, S, QK_NOPE), f32), sin_c, sin_c,
                             jnp.zeros((B, S, QK_ROPE), f32)], axis=-1)

    mod = _ada(c, w_ada[l], b_ada[l]).reshape(B, 9, D)

    x1 = _ffn1(x, mod, row(ln_in_g), row(ln_in_b), ffn1_w_in[l].astype(bf16),
               ffn1_w_out[l].astype(bf16), row(post_ln_g[l, 0]), row(post_ln_b[l, 0]))
    q, k, v_t, yp = _mix(x1, mod, cos_t, sin_t, w_in_pad, row(q_norm_g[l]), row(kv_norm_g[l]),
                         wuq_pad, wuk_pad, wuv_t, wpool, row(pool_b[l]), row(pool_scale[l]))
    attn = _attention(q, k, v_t)
    return _ffn2(x1, attn, yp, mod, woa, wop, row(post_ln_g[l, 1]), row(post_ln_b[l, 1]),
                 ffn2_w_in[l].astype(bf16), ffn2_w_out[l].astype(bf16),
                 row(post_ln_g[l, 2]), row(post_ln_b[l, 2]))
```

```python
import jax
import jax.numpy as jnp
from jax import lax
from jax.experimental import pallas as pl
from jax.experimental.pallas import tpu as pltpu

f32 = jnp.float32
bf16 = jnp.bfloat16

D_MODEL = 1024
D_FF = 2816
N_HEADS = 8
QK_NOPE = 64
QK_ROPE = 32
V_DIM = 64
Q_LORA = 512
KV_LORA = 256
HEAD_PAD = 128
POOL_WINDOWS = (2, 4, 8, 16)
POOL_GROUP = 128
POOL_WIDTH = 512
POOL_HALO = 16
CHUNK = 64
ROPE_THETA = 10000.0
LN_EPS = 1e-5
RMS_EPS = 1e-6
ALPHA = 2.0 ** 0.25
SM_SCALE = (QK_NOPE + QK_ROPE) ** -0.5
LOG2_E = 1.4426950408889634
V_SLAB = 80
IN_PAD = Q_LORA + KV_LORA + HEAD_PAD + POOL_WIDTH

TM_FFN = 1024
TM_SUB = 256
TM_MIX = 1024
TQ = 256
TK = 256
FF_CHUNKS = (1024, 1024, 768)
VMEM_LIMIT = 56 * 1024 * 1024

_NT = (((1,), (1,)), ((), ()))


def _layernorm(x, g, b):
    mu = jnp.mean(x, axis=-1, keepdims=True)
    xc = x - mu
    var = jnp.mean(xc * xc, axis=-1, keepdims=True)
    return xc * lax.rsqrt(var + LN_EPS) * g + b


def _rmsnorm(x, g):
    ms = jnp.mean(x * x, axis=-1, keepdims=True)
    return x * lax.rsqrt(ms + RMS_EPS) * g


def _swiglu(h, w_in_ref, w_out_ref):
    y = None
    off = 0
    for n in FF_CHUNKS:
        g = jnp.dot(h, w_in_ref[:, off:off + n], preferred_element_type=f32)
        u = jnp.dot(h, w_in_ref[:, D_FF + off:D_FF + off + n], preferred_element_type=f32)
        a = (g / (1.0 + jnp.exp(-g)) * u).astype(bf16)
        part = jnp.dot(a, w_out_ref[off:off + n, :], preferred_element_type=f32)
        y = part if y is None else y + part
        off += n
    return y


def _ada_kernel(c_ref, w_ref, b_ref, o_ref):
    c = c_ref[...]
    c_act = (c / (1.0 + jnp.exp(-c))).astype(bf16)
    o_ref[...] = jnp.dot(c_act, w_ref[...].astype(bf16), preferred_element_type=f32) + b_ref[...]


def _ada(c, w_ada, b_ada):
    B = c.shape[0]
    n_out = w_ada.shape[1]
    bn = D_MODEL
    return pl.pallas_call(
        _ada_kernel,
        out_shape=jax.ShapeDtypeStruct((B, n_out), f32),
        grid=(n_out // bn,),
        in_specs=[pl.BlockSpec((B, D_MODEL), lambda j: (0, 0)),
                  pl.BlockSpec((D_MODEL, bn), lambda j: (0, j)),
                  pl.BlockSpec((1, bn), lambda j: (0, j))],
        out_specs=pl.BlockSpec((B, bn), lambda j: (0, j)),
        name="ada",
    )(c, w_ada, b_ada.reshape(1, n_out))


def _ffn_block(x, mod_ref, first_row, w_in_ref, w_out_ref, pg_ref, pb_ref):
    shift, scale, gate = (mod_ref[0, first_row + r:first_row + r + 1, :] for r in range(3))
    h = (x * (1.0 + scale) + shift).astype(bf16)
    y = _swiglu(h, w_in_ref, w_out_ref)
    return _layernorm(ALPHA * x + 0.5 * gate * y, pg_ref[...], pb_ref[...])


def _ffn1_kernel(x_ref, mod_ref, lng_ref, lnb_ref, w_in_ref, w_out_ref, pg_ref, pb_ref, o_ref):
    for r0 in range(0, x_ref.shape[1], TM_SUB):
        rows = slice(r0, r0 + TM_SUB)
        x = _layernorm(x_ref[0, rows, :], lng_ref[...], lnb_ref[...])
        o_ref[0, rows, :] = _ffn_block(x, mod_ref, 0, w_in_ref, w_out_ref, pg_ref, pb_ref)


def _ffn2_kernel(x_ref, attn_ref, yp_ref, mod_ref, woa_ref, wop_ref, pg1_ref, pb1_ref,
                 w_in_ref, w_out_ref, pg2_ref, pb2_ref, o_ref):
    gate1 = mod_ref[0, 5:6, :]
    for r0 in range(0, x_ref.shape[1], TM_SUB):
        rows = slice(r0, r0 + TM_SUB)
        y = (jnp.dot(attn_ref[0, rows, :], woa_ref[...], preferred_element_type=f32)
             + jnp.dot(yp_ref[0, rows, :], wop_ref[...], preferred_element_type=f32))
        x = _layernorm(ALPHA * x_ref[0, rows, :] + gate1 * y, pg1_ref[...], pb1_ref[...])
        o_ref[0, rows, :] = _ffn_block(x, mod_ref, 6, w_in_ref, w_out_ref, pg2_ref, pb2_ref)


def _resident(shape):
    return pl.BlockSpec(shape, lambda b, i: (0,) * len(shape), pipeline_mode=pl.Buffered(1))


def _row_spec(tm, width):
    return pl.BlockSpec((1, tm, width), lambda b, i: (b, i, 0))


def _mod_spec():
    return pl.BlockSpec((1, 9, D_MODEL), lambda b, i: (b, 0, 0))


def _ffn1(x, mod, lng, lnb, w_in, w_out, pg, pb):
    B, S, D = x.shape
    vec = _resident((1, D))
    return pl.pallas_call(
        _ffn1_kernel,
        out_shape=jax.ShapeDtypeStruct((B, S, D), f32),
        grid=(B, S // TM_FFN),
        in_specs=[_row_spec(TM_FFN, D), _mod_spec(), vec, vec,
                  _resident(w_in.shape), _resident(w_out.shape), vec, vec],
        out_specs=_row_spec(TM_FFN, D),
        compiler_params=pltpu.CompilerParams(
            dimension_semantics=("parallel", "parallel"), vmem_limit_bytes=VMEM_LIMIT),
        name="ffn1",
    )(x, mod, lng, lnb, w_in, w_out, pg, pb)


def _ffn2(x1, attn, yp, mod, woa, wop, pg1, pb1, w_in, w_out, pg2, pb2):
    B, S, D = x1.shape
    vec = _resident((1, D))
    return pl.pallas_call(
        _ffn2_kernel,
        out_shape=jax.ShapeDtypeStruct((B, S, D), f32),
        grid=(B, S // TM_FFN),
        in_specs=[_row_spec(TM_FFN, D), _row_spec(TM_FFN, attn.shape[-1]),
                  _row_spec(TM_FFN, yp.shape[-1]), _mod_spec(),
                  _resident(woa.shape), _resident(wop.shape), vec, vec,
                  _resident(w_in.shape), _resident(w_out.shape), vec, vec],
        out_specs=_row_spec(TM_FFN, D),
        compiler_params=pltpu.CompilerParams(
            dimension_semantics=("parallel", "parallel"), vmem_limit_bytes=VMEM_LIMIT),
        name="ffn2",
    )(x1, attn, yp, mod, woa, wop, pg1, pb1, w_in, w_out, pg2, pb2)


def _apply_rope(t, cos_t, sin_t):
    return t * cos_t + pltpu.roll(t, HEAD_PAD - QK_ROPE, 1) * sin_t


def _rope_lane_tables(pos_row, freq_col):
    t = pos_row.shape[1]
    ang = freq_col * pos_row
    c, s = jnp.cos(ang), jnp.sin(ang)
    zeros = jnp.zeros((QK_ROPE, t), f32)
    cos_rows = jnp.concatenate([jnp.ones((QK_NOPE, t), f32), c, c, zeros], axis=0)
    sin_rows = jnp.concatenate([jnp.zeros((QK_NOPE, t), f32), s, s, zeros], axis=0)
    return cos_rows.T, sin_rows.T


def _mix_kernel(x_ref, mod_ref, pos_ref, freq_ref, w_in_ref, qg_ref, kvg_ref, wuq_ref, wuk_ref,
                wuvt_ref, wpool_ref, pbias_ref, pscale_ref,
                q_ref, k_ref, vt_ref, yp_ref, tail_ref):
    st = pl.program_id(1)
    tm = x_ref.shape[1]
    shift, scale = mod_ref[0, 3:4, :], mod_ref[0, 4:5, :]
    ones_rows = (lax.broadcasted_iota(jnp.int32, (V_SLAB - V_DIM, TK), 0) == 0).astype(bf16)

    @pl.when(st == 0)
    def _():
        tail_ref[...] = jnp.zeros_like(tail_ref)

    tail = tail_ref[...]
    for t in range(tm // TK):
        rows = slice(t * TK, (t + 1) * TK)
        h = (x_ref[0, rows, :] * (1.0 + scale) + shift).astype(bf16)
        proj = jnp.dot(h, w_in_ref[...], preferred_element_type=f32)
        cq = proj[:, :Q_LORA]
        ckv = proj[:, Q_LORA:Q_LORA + KV_LORA]
        krb = proj[:, Q_LORA + KV_LORA:Q_LORA + KV_LORA + HEAD_PAD]
        u = proj[:, Q_LORA + KV_LORA + HEAD_PAD:]
        cos_t, sin_t = _rope_lane_tables(pos_ref[0, :, rows].astype(f32), freq_ref[...])

        cqn = _rmsnorm(cq, qg_ref[...]).astype(bf16)
        q_all = jnp.dot(cqn, wuq_ref[...], preferred_element_type=f32)
        ckvn = _rmsnorm(ckv, kvg_ref[...]).astype(bf16)
        k_all = jnp.dot(ckvn, wuk_ref[...], preferred_element_type=f32)
        v_t = lax.dot_general(wuvt_ref[...], ckvn, _NT, preferred_element_type=f32).astype(bf16)
        for hh in range(N_HEADS):
            vt_ref[0, t, hh * V_SLAB:hh * V_SLAB + V_DIM, :] = v_t[hh * V_DIM:(hh + 1) * V_DIM, :]
            vt_ref[0, t, hh * V_SLAB + V_DIM:(hh + 1) * V_SLAB, :] = ones_rows
        k_rope = _apply_rope(krb, cos_t, sin_t)
        for hh in range(N_HEADS):
            sl = slice(hh * HEAD_PAD, (hh + 1) * HEAD_PAD)
            q_ref[0, rows, sl] = (_apply_rope(q_all[:, sl], cos_t, sin_t)
                                  * (SM_SCALE * LOG2_E)).astype(bf16)
            k_ref[0, rows, sl] = (k_all[:, sl] + k_rope).astype(bf16)

        ext = jnp.concatenate([tail, u], axis=0)
        tail = u[TK - POOL_HALO:, :]
        t_idx = st * tm + t * TK + lax.broadcasted_iota(jnp.int32, (TK, 1), 0)
        groups = []
        for gi, w in enumerate(POOL_WINDOWS):
            e = ext[:, gi * POOL_GROUP:(gi + 1) * POOL_GROUP]
            sh = 1
            while sh < w:
                e = e + pltpu.roll(e, sh, 0)
                sh *= 2
            cnt = jnp.minimum(t_idx + 1, w).astype(f32)
            groups.append(e[POOL_HALO:, :] / cnt - u[:, gi * POOL_GROUP:(gi + 1) * POOL_GROUP])
        pooled = jnp.concatenate(groups, axis=1).astype(bf16)
        yp = jnp.dot(pooled, wpool_ref[...], preferred_element_type=f32) + pbias_ref[...]
        yp_ref[0, rows, :] = (yp * pscale_ref[...]).astype(bf16)
    tail_ref[...] = tail


def _mix(x1, mod, pos, freq, w_in, qg, kvg, wuq, wuk, wuvt, wpool, pbias, pscale):
    B, S, D = x1.shape
    tm = TM_MIX
    hw = N_HEADS * HEAD_PAD
    vw = N_HEADS * V_SLAB

    def const(shape):
        return pl.BlockSpec(shape, lambda b, i: (0,) * len(shape))

    return pl.pallas_call(
        _mix_kernel,
        out_shape=(jax.ShapeDtypeStruct((B, S, hw), bf16),
                   jax.ShapeDtypeStruct((B, S, hw), bf16),
                   jax.ShapeDtypeStruct((B, S // TK, vw, TK), bf16),
                   jax.ShapeDtypeStruct((B, S, POOL_WIDTH), bf16)),
        grid=(B, S // tm),
        in_specs=[_row_spec(tm, D), _mod_spec(),
                  pl.BlockSpec((1, 1, tm), lambda b, i: (b, 0, i)), const(freq.shape),
                  const(w_in.shape), const(qg.shape), const(kvg.shape), const(wuq.shape),
                  const(wuk.shape), const(wuvt.shape), const(wpool.shape), const(pbias.shape),
                  const(pscale.shape)],
        out_specs=(_row_spec(tm, hw), _row_spec(tm, hw),
                   pl.BlockSpec((1, tm // TK, vw, TK), lambda b, i: (b, i, 0, 0)),
                   _row_spec(tm, POOL_WIDTH)),
        scratch_shapes=[pltpu.VMEM((POOL_HALO, POOL_WIDTH), f32)],
        compiler_params=pltpu.CompilerParams(
            dimension_semantics=("parallel", "arbitrary"), vmem_limit_bytes=VMEM_LIMIT),
        name="mix_proj",
    )(x1, mod, pos, freq, w_in, qg, kvg, wuq, wuk, wuvt, wpool, pbias, pscale)


def _attn_kernel(q_ref, k_ref, vt_ref, o_ref, m_ref, acc_ref, s_a, s_b, mt_a, mt_b):
    i = pl.program_id(1)
    m_ref[...] = jnp.full_like(m_ref, -jnp.inf)
    acc_ref[...] = jnp.zeros_like(acc_ref)

    def qk_head(hh, j, s_buf, mt_buf, diag):
        koff = pl.multiple_of(j * TK, TK)
        sl = slice(hh * HEAD_PAD, (hh + 1) * HEAD_PAD)
        s = lax.dot_general(k_ref[0, pl.ds(koff, TK), sl], q_ref[0, :, sl], _NT,
                            preferred_element_type=f32)
        if diag is not None:
            key_chunk = lax.broadcasted_iota(jnp.int32, (TK, TQ), 0) // CHUNK + diag * (TK // CHUNK)
            qry_chunk = lax.broadcasted_iota(jnp.int32, (TK, TQ), 1) // CHUNK
            s = jnp.where(key_chunk <= qry_chunk, s, -jnp.inf)
        s_buf[hh] = s
        mt_buf[hh] = jnp.max(s, axis=0, keepdims=True)

    def pv_head(hh, j, s_buf, mt_buf):
        m_prev = m_ref[hh]
        m_new = jnp.maximum(m_prev, mt_buf[hh])
        alpha = jnp.exp2(m_prev - m_new)
        p = jnp.exp2(s_buf[hh] - m_new).astype(bf16)
        v_t = vt_ref[0, j, hh * V_SLAB:(hh + 1) * V_SLAB, :]
        acc_ref[hh] = alpha * acc_ref[hh] + jnp.dot(v_t, p, preferred_element_type=f32)
        m_ref[hh] = m_new

    def stage(qk=None, pv=None):
        for hh in range(N_HEADS):
            if qk is not None:
                qk_head(hh, *qk)
            if pv is not None:
                pv_head(hh, *pv)

    stage(qk=(i, s_a, mt_a, 0))
    n_pairs = i // 2

    def pair(u):
        stage(qk=(2 * u, s_b, mt_b, None), pv=(jnp.where(u == 0, i, 2 * u - 1), s_a, mt_a))
        stage(qk=(2 * u + 1, s_a, mt_a, None), pv=(2 * u, s_b, mt_b))

    def body(v, carry):
        pair(2 * v)
        pair(2 * v + 1)
        return carry

    lax.fori_loop(0, n_pairs // 2, body, 0)

    @pl.when(n_pairs % 2 == 1)
    def _():
        pair(n_pairs - 1)

    tile_a = jnp.where(n_pairs == 0, i, 2 * n_pairs - 1)

    @pl.when(i % 2 == 1)
    def _():
        stage(qk=(i - 1, s_b, mt_b, None), pv=(tile_a, s_a, mt_a))
        stage(pv=(i - 1, s_b, mt_b))

    @pl.when(i % 2 == 0)
    def _():
        stage(pv=(tile_a, s_a, mt_a))

    o_t = jnp.concatenate([acc_ref[hh, :V_DIM, :] / acc_ref[hh, V_DIM:V_DIM + 1, :]
                           for hh in range(N_HEADS)], axis=0)
    o_ref[0] = o_t.T.astype(bf16)


def _attention(q, k, v_t):
    B, S, hw = q.shape
    vw = N_HEADS * V_DIM
    return pl.pallas_call(
        _attn_kernel,
        out_shape=jax.ShapeDtypeStruct((B, S, vw), bf16),
        grid=(B, S // TQ),
        in_specs=[_row_spec(TQ, hw),
                  pl.BlockSpec((1, S, hw), lambda b, i: (b, 0, 0)),
                  pl.BlockSpec((1,) + v_t.shape[1:], lambda b, i: (b, 0, 0, 0))],
        out_specs=_row_spec(TQ, vw),
        scratch_shapes=[pltpu.VMEM((N_HEADS, 1, TQ), f32),
                        pltpu.VMEM((N_HEADS, V_SLAB, TQ), f32),
                        pltpu.VMEM((N_HEADS, TK, TQ), f32),
                        pltpu.VMEM((N_HEADS, TK, TQ), f32),
                        pltpu.VMEM((N_HEADS, 1, TQ), f32),
                        pltpu.VMEM((N_HEADS, 1, TQ), f32)],
        compiler_params=pltpu.CompilerParams(
            dimension_semantics=("parallel", "parallel"), vmem_limit_bytes=VMEM_LIMIT),
        name="attention",
    )(q, k, v_t)


def _rot_cols(w):
    half = w.shape[-1] // 2
    return jnp.concatenate([-w[..., half:], w[..., :half]], axis=-1)


def _head_slots(parts):
    k_dim = parts[0][0].shape[0]
    cols = []
    for blocks in parts:
        used = sum(b.shape[1] for b in blocks)
        cols.extend(blocks)
        if used < HEAD_PAD:
            cols.append(jnp.zeros((k_dim, HEAD_PAD - used), f32))
    return jnp.concatenate(cols, axis=1)


def kernel(x, c, positions, ln_in_g, ln_in_b, w_ada, b_ada, ffn1_w_in, ffn1_w_out, w_in, q_norm_g, kv_norm_g, w_uq, w_ukv, pool_w, pool_b, pool_scale, w_out, ffn2_w_in, ffn2_w_out, post_ln_g, post_ln_b):
    B, S, D = x.shape
    l = 0
    row = lambda a: a.reshape(1, -1)

    qh = QK_NOPE + QK_ROPE
    wuq = w_uq[l]
    wuq_pad = _head_slots([[wuq[:, h * qh:h * qh + QK_NOPE],
                            wuq[:, h * qh + QK_NOPE:(h + 1) * qh],
                            _rot_cols(wuq[:, h * qh + QK_NOPE:(h + 1) * qh])]
                           for h in range(N_HEADS)]).astype(bf16)
    kvh = QK_NOPE + V_DIM
    wukv = w_ukv[l]
    wuk_pad = _head_slots([[wukv[:, h * kvh:h * kvh + QK_NOPE]] for h in range(N_HEADS)]).astype(bf16)
    wuv_t = jnp.concatenate([wukv[:, h * kvh + QK_NOPE:(h + 1) * kvh] for h in range(N_HEADS)],
                            axis=1).T.astype(bf16)
    w_kr = w_in[l][:, Q_LORA + KV_LORA:Q_LORA + KV_LORA + QK_ROPE]
    w_in_pad = jnp.concatenate(
        [w_in[l][:, :Q_LORA + KV_LORA], jnp.zeros((D, QK_NOPE), f32), w_kr, _rot_cols(w_kr),
         w_in[l][:, Q_LORA + KV_LORA + QK_ROPE:]], axis=1).astype(bf16)
    wpool = jax.scipy.linalg.block_diag(*[pool_w[l, g] for g in range(len(POOL_WINDOWS))]).astype(bf16)
    wo = w_out[l]
    woa = wo[:N_HEADS * V_DIM].astype(bf16)
    wop = wo[N_HEADS * V_DIM:].astype(bf16)

    inv_freq = ROPE_THETA ** (-jnp.arange(0, QK_ROPE, 2, dtype=f32) / QK_ROPE)

    mod = _ada(c, w_ada[l], b_ada[l]).reshape(B, 9, D)

    x1 = _ffn1(x, mod, row(ln_in_g), row(ln_in_b), ffn1_w_in[l].astype(bf16),
               ffn1_w_out[l].astype(bf16), row(post_ln_g[l, 0]), row(post_ln_b[l, 0]))
    q, k, v_t, yp = _mix(x1, mod, positions.reshape(B, 1, S), inv_freq.reshape(-1, 1), w_in_pad,
                         row(q_norm_g[l]), row(kv_norm_g[l]),
                         wuq_pad, wuk_pad, wuv_t, wpool, row(pool_b[l]), row(pool_scale[l]))
    attn = _attention(q, k, v_t)
    return _ffn2(x1, attn, yp, mod, woa, wop, row(post_ln_g[l, 1]), row(post_ln_b[l, 1]),
                 ffn2_w_in[l].astype(bf16), ffn2_w_out[l].astype(bf16),
                 row(post_ln_g[l, 2]), row(post_ln_b[l, 2]))
```

```python
import jax
import jax.numpy as jnp
from jax import lax
from jax.experimental import pallas as pl
from jax.experimental.pallas import tpu as pltpu

f32 = jnp.float32
bf16 = jnp.bfloat16

D_MODEL = 1024
D_FF = 2816
N_HEADS = 8
QK_NOPE = 64
QK_ROPE = 32
V_DIM = 64
Q_LORA = 512
KV_LORA = 256
HEAD_PAD = 128
POOL_WINDOWS = (2, 4, 8, 16)
POOL_GROUP = 128
POOL_WIDTH = 512
POOL_HALO = 16
CHUNK = 64
ROPE_THETA = 10000.0
LN_EPS = 1e-5
RMS_EPS = 1e-6
ALPHA = 2.0 ** 0.25
SM_SCALE = (QK_NOPE + QK_ROPE) ** -0.5
LOG2_E = 1.4426950408889634
V_SLAB = 80
IN_PAD = Q_LORA + KV_LORA + HEAD_PAD + POOL_WIDTH

TM_FFN = 1024
TM_SUB = 256
TM_MIX = 1024
TQ = 256
TK = 256
FF_CHUNKS = (1024, 1024, 768)
VMEM_LIMIT = 56 * 1024 * 1024

_NT = (((1,), (1,)), ((), ()))


def _layernorm(x, g, b):
    mu = jnp.mean(x, axis=-1, keepdims=True)
    xc = x - mu
    var = jnp.mean(xc * xc, axis=-1, keepdims=True)
    return xc * lax.rsqrt(var + LN_EPS) * g + b


def _rmsnorm(x, g):
    ms = jnp.mean(x * x, axis=-1, keepdims=True)
    return x * lax.rsqrt(ms + RMS_EPS) * g


def _swiglu(h, w_in_ref, w_out_ref):
    y = None
    off = 0
    for n in FF_CHUNKS:
        g = jnp.dot(h, w_in_ref[:, off:off + n], preferred_element_type=f32)
        u = jnp.dot(h, w_in_ref[:, D_FF + off:D_FF + off + n], preferred_element_type=f32)
        a = (g / (1.0 + jnp.exp(-g)) * u).astype(bf16)
        part = jnp.dot(a, w_out_ref[off:off + n, :], preferred_element_type=f32)
        y = part if y is None else y + part
        off += n
    return y


def _ada_kernel(c_ref, w_ref, b_ref, o_ref):
    c = c_ref[...]
    c_act = (c / (1.0 + jnp.exp(-c))).astype(bf16)
    o_ref[...] = jnp.dot(c_act, w_ref[...].astype(bf16), preferred_element_type=f32) + b_ref[...]


def _ada(c, w_ada, b_ada):
    B = c.shape[0]
    n_out = w_ada.shape[1]
    bn = D_MODEL
    return pl.pallas_call(
        _ada_kernel,
        out_shape=jax.ShapeDtypeStruct((B, n_out), f32),
        grid=(n_out // bn,),
        in_specs=[pl.BlockSpec((B, D_MODEL), lambda j: (0, 0)),
                  pl.BlockSpec((D_MODEL, bn), lambda j: (0, j)),
                  pl.BlockSpec((1, bn), lambda j: (0, j))],
        out_specs=pl.BlockSpec((B, bn), lambda j: (0, j)),
        name="ada",
    )(c, w_ada, b_ada.reshape(1, n_out))


def _ffn_block(x, mod_ref, first_row, w_in_ref, w_out_ref, pg_ref, pb_ref):
    shift, scale, gate = (mod_ref[0, first_row + r:first_row + r + 1, :] for r in range(3))
    h = (x * (1.0 + scale) + shift).astype(bf16)
    y = _swiglu(h, w_in_ref, w_out_ref)
    return _layernorm(ALPHA * x + 0.5 * gate * y, pg_ref[...], pb_ref[...])


def _ffn1_kernel(x_ref, mod_ref, lng_ref, lnb_ref, w_in_ref, w_out_ref, pg_ref, pb_ref, o_ref):
    for r0 in range(0, x_ref.shape[1], TM_SUB):
        rows = slice(r0, r0 + TM_SUB)
        x = _layernorm(x_ref[0, rows, :], lng_ref[...], lnb_ref[...])
        o_ref[0, rows, :] = _ffn_block(x, mod_ref, 0, w_in_ref, w_out_ref, pg_ref, pb_ref)


def _ffn2_kernel(x_ref, attn_ref, yp_ref, mod_ref, woa_ref, wop_ref, pg1_ref, pb1_ref,
                 w_in_ref, w_out_ref, pg2_ref, pb2_ref, o_ref):
    gate1 = mod_ref[0, 5:6, :]
    for r0 in range(0, x_ref.shape[1], TM_SUB):
        rows = slice(r0, r0 + TM_SUB)
        y = (jnp.dot(attn_ref[0, rows, :], woa_ref[...], preferred_element_type=f32)
             + jnp.dot(yp_ref[0, rows, :], wop_ref[...], preferred_element_type=f32))
        x = _layernorm(ALPHA * x_ref[0, rows, :] + gate1 * y, pg1_ref[...], pb1_ref[...])
        o_ref[0, rows, :] = _ffn_block(x, mod_ref, 6, w_in_ref, w_out_ref, pg2_ref, pb2_ref)


def _resident(shape):
    return pl.BlockSpec(shape, lambda b, i: (0,) * len(shape), pipeline_mode=pl.Buffered(1))


def _row_spec(tm, width):
    return pl.BlockSpec((1, tm, width), lambda b, i: (b, i, 0))


def _mod_spec():
    return pl.BlockSpec((1, 9, D_MODEL), lambda b, i: (b, 0, 0))


def _ffn1(x, mod, lng, lnb, w_in, w_out, pg, pb):
    B, S, D = x.shape
    vec = _resident((1, D))
    return pl.pallas_call(
        _ffn1_kernel,
        out_shape=jax.ShapeDtypeStruct((B, S, D), f32),
        grid=(B, S // TM_FFN),
        in_specs=[_row_spec(TM_FFN, D), _mod_spec(), vec, vec,
                  _resident(w_in.shape), _resident(w_out.shape), vec, vec],
        out_specs=_row_spec(TM_FFN, D),
        compiler_params=pltpu.CompilerParams(
            dimension_semantics=("parallel", "parallel"), vmem_limit_bytes=VMEM_LIMIT),
        name="ffn1",
    )(x, mod, lng, lnb, w_in, w_out, pg, pb)


def _ffn2(x1, attn, yp, mod, woa, wop, pg1, pb1, w_in, w_out, pg2, pb2):
    B, S, D = x1.shape
    vec = _resident((1, D))
    return pl.pallas_call(
        _ffn2_kernel,
        out_shape=jax.ShapeDtypeStruct((B, S, D), f32),
        grid=(B, S // TM_FFN),
        in_specs=[_row_spec(TM_FFN, D), _row_spec(TM_FFN, attn.shape[-1]),
                  _row_spec(TM_FFN, yp.shape[-1]), _mod_spec(),
                  _resident(woa.shape), _resident(wop.shape), vec, vec,
                  _resident(w_in.shape), _resident(w_out.shape), vec, vec],
        out_specs=_row_spec(TM_FFN, D),
        compiler_params=pltpu.CompilerParams(
            dimension_semantics=("parallel", "parallel"), vmem_limit_bytes=VMEM_LIMIT),
        name="ffn2",
    )(x1, attn, yp, mod, woa, wop, pg1, pb1, w_in, w_out, pg2, pb2)


def _apply_rope(t, cos_t, sin_t):
    return t * cos_t + pltpu.roll(t, HEAD_PAD - QK_ROPE, 1) * sin_t


def _rope_slot_tables(pos_row, freq_col):
    t = pos_row.shape[1]
    ang = freq_col * pos_row
    c, s = jnp.cos(ang), jnp.sin(ang)
    zeros = jnp.zeros((QK_ROPE, t), f32)
    cos_rows = jnp.concatenate([jnp.ones((QK_NOPE, t), f32), c, c, zeros], axis=0)
    sin_rows = jnp.concatenate([jnp.zeros((QK_NOPE, t), f32), s, s, zeros], axis=0)
    return cos_rows, sin_rows


def _mix_kernel(x_ref, mod_ref, pos_ref, freq_ref, w_in_ref, qg_ref, kvg_ref, wuqt_ref, wuk_ref,
                wuvt_ref, wpool_ref, pbias_ref, pscale_ref,
                qt_ref, k_ref, vt_ref, yp_ref, tail_ref):
    st = pl.program_id(1)
    tm = x_ref.shape[1]
    shift, scale = mod_ref[0, 3:4, :], mod_ref[0, 4:5, :]
    ones_rows = (lax.broadcasted_iota(jnp.int32, (V_SLAB - V_DIM, TK), 0) == 0).astype(bf16)

    @pl.when(st == 0)
    def _():
        tail_ref[...] = jnp.zeros_like(tail_ref)

    tail = tail_ref[...]
    for t in range(tm // TK):
        rows = slice(t * TK, (t + 1) * TK)
        h = (x_ref[0, rows, :] * (1.0 + scale) + shift).astype(bf16)
        proj = jnp.dot(h, w_in_ref[...], preferred_element_type=f32)
        cq = proj[:, :Q_LORA]
        ckv = proj[:, Q_LORA:Q_LORA + KV_LORA]
        krb = proj[:, Q_LORA + KV_LORA:Q_LORA + KV_LORA + HEAD_PAD]
        u = proj[:, Q_LORA + KV_LORA + HEAD_PAD:]
        cos_rows, sin_rows = _rope_slot_tables(pos_ref[0, :, rows].astype(f32), freq_ref[...])
        cos_t, sin_t = cos_rows.T, sin_rows.T

        cqn = _rmsnorm(cq, qg_ref[...]).astype(bf16)
        q_t = lax.dot_general(wuqt_ref[...], cqn, _NT, preferred_element_type=f32)
        for hh in range(N_HEADS):
            sl = slice(hh * HEAD_PAD, (hh + 1) * HEAD_PAD)
            blk = q_t[sl, :]
            blk = blk * cos_rows + pltpu.roll(blk, HEAD_PAD - QK_ROPE, 0) * sin_rows
            qt_ref[0, t, sl, :] = (blk * (SM_SCALE * LOG2_E)).astype(bf16)
        ckvn = _rmsnorm(ckv, kvg_ref[...]).astype(bf16)
        k_all = jnp.dot(ckvn, wuk_ref[...], preferred_element_type=f32)
        v_t = lax.dot_general(wuvt_ref[...], ckvn, _NT, preferred_element_type=f32).astype(bf16)
        for hh in range(N_HEADS):
            vt_ref[0, t, hh * V_SLAB:hh * V_SLAB + V_DIM, :] = v_t[hh * V_DIM:(hh + 1) * V_DIM, :]
            vt_ref[0, t, hh * V_SLAB + V_DIM:(hh + 1) * V_SLAB, :] = ones_rows
        k_rope = _apply_rope(krb, cos_t, sin_t)
        for hh in range(N_HEADS):
            sl = slice(hh * HEAD_PAD, (hh + 1) * HEAD_PAD)
            k_ref[0, rows, sl] = (k_all[:, sl] + k_rope).astype(bf16)

        ext = jnp.concatenate([tail, u], axis=0)
        tail = u[TK - POOL_HALO:, :]
        t_idx = st * tm + t * TK + lax.broadcasted_iota(jnp.int32, (TK, 1), 0)
        groups = []
        for gi, w in enumerate(POOL_WINDOWS):
            e = ext[:, gi * POOL_GROUP:(gi + 1) * POOL_GROUP]
            sh = 1
            while sh < w:
                e = e + pltpu.roll(e, sh, 0)
                sh *= 2
            cnt = jnp.minimum(t_idx + 1, w).astype(f32)
            groups.append(e[POOL_HALO:, :] / cnt - u[:, gi * POOL_GROUP:(gi + 1) * POOL_GROUP])
        pooled = jnp.concatenate(groups, axis=1).astype(bf16)
        yp = jnp.dot(pooled, wpool_ref[...], preferred_element_type=f32) + pbias_ref[...]
        yp_ref[0, rows, :] = (yp * pscale_ref[...]).astype(bf16)
    tail_ref[...] = tail


def _mix(x1, mod, pos, freq, w_in, qg, kvg, wuq, wuk, wuvt, wpool, pbias, pscale):
    B, S, D = x1.shape
    assert TQ == TK
    tm = TM_MIX
    hw = N_HEADS * HEAD_PAD
    vw = N_HEADS * V_SLAB

    def const(shape):
        return pl.BlockSpec(shape, lambda b, i: (0,) * len(shape))

    return pl.pallas_call(
        _mix_kernel,
        out_shape=(jax.ShapeDtypeStruct((B, S // TQ, hw, TQ), bf16),
                   jax.ShapeDtypeStruct((B, S, hw), bf16),
                   jax.ShapeDtypeStruct((B, S // TK, vw, TK), bf16),
                   jax.ShapeDtypeStruct((B, S, POOL_WIDTH), bf16)),
        grid=(B, S // tm),
        in_specs=[_row_spec(tm, D), _mod_spec(),
                  pl.BlockSpec((1, 1, tm), lambda b, i: (b, 0, i)), const(freq.shape),
                  const(w_in.shape), const(qg.shape), const(kvg.shape), const(wuq.shape),
                  const(wuk.shape), const(wuvt.shape), const(wpool.shape), const(pbias.shape),
                  const(pscale.shape)],
        out_specs=(pl.BlockSpec((1, tm // TQ, hw, TQ), lambda b, i: (b, i, 0, 0)),
                   _row_spec(tm, hw),
                   pl.BlockSpec((1, tm // TK, vw, TK), lambda b, i: (b, i, 0, 0)),
                   _row_spec(tm, POOL_WIDTH)),
        scratch_shapes=[pltpu.VMEM((POOL_HALO, POOL_WIDTH), f32)],
        compiler_params=pltpu.CompilerParams(
            dimension_semantics=("parallel", "arbitrary"), vmem_limit_bytes=VMEM_LIMIT),
        name="mix_proj",
    )(x1, mod, pos, freq, w_in, qg, kvg, wuq, wuk, wuvt, wpool, pbias, pscale)


def _attn_kernel(qt_ref, k_ref, vt_ref, o_ref, m_ref, acc_ref, s_a, s_b, mt_a, mt_b):
    i = pl.program_id(1)
    m_ref[...] = jnp.full_like(m_ref, -jnp.inf)
    acc_ref[...] = jnp.zeros_like(acc_ref)

    def qk_head(hh, j, s_buf, mt_buf, diag):
        koff = pl.multiple_of(j * TK, TK)
        sl = slice(hh * HEAD_PAD, (hh + 1) * HEAD_PAD)
        s = jnp.dot(k_ref[0, pl.ds(koff, TK), sl], qt_ref[0, 0, sl, :],
                    preferred_element_type=f32)
        if diag is not None:
            key_chunk = lax.broadcasted_iota(jnp.int32, (TK, TQ), 0) // CHUNK + diag * (TK // CHUNK)
            qry_chunk = lax.broadcasted_iota(jnp.int32, (TK, TQ), 1) // CHUNK
            s = jnp.where(key_chunk <= qry_chunk, s, -jnp.inf)
        s_buf[hh] = s
        mt_buf[hh] = jnp.max(s, axis=0, keepdims=True)

    def pv_head(hh, j, s_buf, mt_buf):
        m_prev = m_ref[hh]
        m_new = jnp.maximum(m_prev, mt_buf[hh])
        alpha = jnp.exp2(m_prev - m_new)
        p = jnp.exp2(s_buf[hh] - m_new).astype(bf16)
        v_t = vt_ref[0, j, hh * V_SLAB:(hh + 1) * V_SLAB, :]
        acc_ref[hh] = alpha * acc_ref[hh] + jnp.dot(v_t, p, preferred_element_type=f32)
        m_ref[hh] = m_new

    def stage(qk=None, pv=None):
        for hh in range(N_HEADS):
            if qk is not None:
                qk_head(hh, *qk)
            if pv is not None:
                pv_head(hh, *pv)

    stage(qk=(i, s_a, mt_a, 0))
    n_pairs = i // 2

    def pair(u):
        stage(qk=(2 * u, s_b, mt_b, None), pv=(jnp.where(u == 0, i, 2 * u - 1), s_a, mt_a))
        stage(qk=(2 * u + 1, s_a, mt_a, None), pv=(2 * u, s_b, mt_b))

    def body(v, carry):
        pair(2 * v)
        pair(2 * v + 1)
        return carry

    lax.fori_loop(0, n_pairs // 2, body, 0)

    @pl.when(n_pairs % 2 == 1)
    def _():
        pair(n_pairs - 1)

    tile_a = jnp.where(n_pairs == 0, i, 2 * n_pairs - 1)

    @pl.when(i % 2 == 1)
    def _():
        stage(qk=(i - 1, s_b, mt_b, None), pv=(tile_a, s_a, mt_a))
        stage(pv=(i - 1, s_b, mt_b))

    @pl.when(i % 2 == 0)
    def _():
        stage(pv=(tile_a, s_a, mt_a))

    o_t = jnp.concatenate([acc_ref[hh, :V_DIM, :] / acc_ref[hh, V_DIM:V_DIM + 1, :]
                           for hh in range(N_HEADS)], axis=0)
    o_ref[0] = o_t.T.astype(bf16)


def _attention(q_t, k, v_t):
    B, S, hw = k.shape
    vw = N_HEADS * V_DIM
    return pl.pallas_call(
        _attn_kernel,
        out_shape=jax.ShapeDtypeStruct((B, S, vw), bf16),
        grid=(B, S // TQ),
        in_specs=[pl.BlockSpec((1, 1, hw, TQ), lambda b, i: (b, i, 0, 0)),
                  pl.BlockSpec((1, S, hw), lambda b, i: (b, 0, 0)),
                  pl.BlockSpec((1,) + v_t.shape[1:], lambda b, i: (b, 0, 0, 0))],
        out_specs=_row_spec(TQ, vw),
        scratch_shapes=[pltpu.VMEM((N_HEADS, 1, TQ), f32),
                        pltpu.VMEM((N_HEADS, V_SLAB, TQ), f32),
                        pltpu.VMEM((N_HEADS, TK, TQ), f32),
                        pltpu.VMEM((N_HEADS, TK, TQ), f32),
                        pltpu.VMEM((N_HEADS, 1, TQ), f32),
                        pltpu.VMEM((N_HEADS, 1, TQ), f32)],
        compiler_params=pltpu.CompilerParams(
            dimension_semantics=("parallel", "parallel"), vmem_limit_bytes=VMEM_LIMIT),
        name="attention",
    )(q_t, k, v_t)


def _rot_cols(w):
    half = w.shape[-1] // 2
    return jnp.concatenate([-w[..., half:], w[..., :half]], axis=-1)


def _head_slots(parts):
    k_dim = parts[0][0].shape[0]
    cols = []
    for blocks in parts:
        used = sum(b.shape[1] for b in blocks)
        cols.extend(blocks)
        if used < HEAD_PAD:
            cols.append(jnp.zeros((k_dim, HEAD_PAD - used), f32))
    return jnp.concatenate(cols, axis=1)


def kernel(x, c, positions, ln_in_g, ln_in_b, w_ada, b_ada, ffn1_w_in, ffn1_w_out, w_in, q_norm_g, kv_norm_g, w_uq, w_ukv, pool_w, pool_b, pool_scale, w_out, ffn2_w_in, ffn2_w_out, post_ln_g, post_ln_b):
    B, S, D = x.shape
    l = 0
    row = lambda a: a.reshape(1, -1)

    qh = QK_NOPE + QK_ROPE
    wuq = w_uq[l]
    wuq_pad = _head_slots([[wuq[:, h * qh:h * qh + QK_NOPE],
                            wuq[:, h * qh + QK_NOPE:(h + 1) * qh],
                            _rot_cols(wuq[:, h * qh + QK_NOPE:(h + 1) * qh])]
                           for h in range(N_HEADS)]).astype(bf16)
    kvh = QK_NOPE + V_DIM
    wukv = w_ukv[l]
    wuk_pad = _head_slots([[wukv[:, h * kvh:h * kvh + QK_NOPE]] for h in range(N_HEADS)]).astype(bf16)
    wuv_t = jnp.concatenate([wukv[:, h * kvh + QK_NOPE:(h + 1) * kvh] for h in range(N_HEADS)],
                            axis=1).T.astype(bf16)
    w_kr = w_in[l][:, Q_LORA + KV_LORA:Q_LORA + KV_LORA + QK_ROPE]
    w_in_pad = jnp.concatenate(
        [w_in[l][:, :Q_LORA + KV_LORA], jnp.zeros((D, QK_NOPE), f32), w_kr, _rot_cols(w_kr),
         w_in[l][:, Q_LORA + KV_LORA + QK_ROPE:]], axis=1).astype(bf16)
    wpool = jax.scipy.linalg.block_diag(*[pool_w[l, g] for g in range(len(POOL_WINDOWS))]).astype(bf16)
    wo = w_out[l]
    woa = wo[:N_HEADS * V_DIM].astype(bf16)
    wop = wo[N_HEADS * V_DIM:].astype(bf16)

    inv_freq = ROPE_THETA ** (-jnp.arange(0, QK_ROPE, 2, dtype=f32) / QK_ROPE)

    mod = _ada(c, w_ada[l], b_ada[l]).reshape(B, 9, D)

    x1 = _ffn1(x, mod, row(ln_in_g), row(ln_in_b), ffn1_w_in[l].astype(bf16),
               ffn1_w_out[l].astype(bf16), row(post_ln_g[l, 0]), row(post_ln_b[l, 0]))
    q_t, k, v_t, yp = _mix(x1, mod, positions.reshape(B, 1, S), inv_freq.reshape(-1, 1), w_in_pad,
                           row(q_norm_g[l]), row(kv_norm_g[l]),
                           wuq_pad.T, wuk_pad, wuv_t, wpool, row(pool_b[l]), row(pool_scale[l]))
    attn = _attention(q_t, k, v_t)
    return _ffn2(x1, attn, yp, mod, woa, wop, row(post_ln_g[l, 1]), row(post_ln_b[l, 1]),
                 ffn2_w_in[l].astype(bf16), ffn2_w_out[l].astype(bf16),
                 row(post_ln_g[l, 2]), row(post_ln_b[l, 2]))
```

```python
import jax
import jax.numpy as jnp
from jax import lax
from jax.experimental import pallas as pl
from jax.experimental.pallas import tpu as pltpu

f32 = jnp.float32
bf16 = jnp.bfloat16

D_MODEL = 1024
D_FF = 2816
N_HEADS = 8
QK_NOPE = 64
QK_ROPE = 32
V_DIM = 64
Q_LORA = 512
KV_LORA = 256
HEAD_PAD = 128
POOL_WINDOWS = (2, 4, 8, 16)
POOL_GROUP = 128
POOL_WIDTH = 512
POOL_HALO = 16
CHUNK = 64
ROPE_THETA = 10000.0
LN_EPS = 1e-5
RMS_EPS = 1e-6
ALPHA = 2.0 ** 0.25
SM_SCALE = (QK_NOPE + QK_ROPE) ** -0.5
LOG2_E = 1.4426950408889634
V_SLAB = 80
IN_PAD = Q_LORA + KV_LORA + HEAD_PAD + POOL_WIDTH

TM_FFN = 1024
TM_SUB = 256
TM_MIX = 1024
TQ = 256
TK = 256
Q_PER_STEP = 2
FF_CHUNKS = (1024, 1024, 768)
VMEM_LIMIT = 56 * 1024 * 1024

_NT = (((1,), (1,)), ((), ()))


def _layernorm(x, g, b):
    mu = jnp.mean(x, axis=-1, keepdims=True)
    xc = x - mu
    var = jnp.mean(xc * xc, axis=-1, keepdims=True)
    return xc * lax.rsqrt(var + LN_EPS) * g + b


def _rmsnorm(x, g):
    ms = jnp.mean(x * x, axis=-1, keepdims=True)
    return x * lax.rsqrt(ms + RMS_EPS) * g


def _swiglu(h, w_in_ref, w_out_ref):
    y = None
    off = 0
    for n in FF_CHUNKS:
        g = jnp.dot(h, w_in_ref[:, off:off + n], preferred_element_type=f32)
        u = jnp.dot(h, w_in_ref[:, D_FF + off:D_FF + off + n], preferred_element_type=f32)
        a = (g / (1.0 + jnp.exp(-g)) * u).astype(bf16)
        part = jnp.dot(a, w_out_ref[off:off + n, :], preferred_element_type=f32)
        y = part if y is None else y + part
        off += n
    return y


def _ada_kernel(c_ref, w_ref, b_ref, o_ref):
    c = c_ref[...]
    c_act = (c / (1.0 + jnp.exp(-c))).astype(bf16)
    o_ref[...] = jnp.dot(c_act, w_ref[...].astype(bf16), preferred_element_type=f32) + b_ref[...]


def _ada(c, w_ada, b_ada):
    B = c.shape[0]
    n_out = w_ada.shape[1]
    bn = D_MODEL
    return pl.pallas_call(
        _ada_kernel,
        out_shape=jax.ShapeDtypeStruct((B, n_out), f32),
        grid=(n_out // bn,),
        in_specs=[pl.BlockSpec((B, D_MODEL), lambda j: (0, 0)),
                  pl.BlockSpec((D_MODEL, bn), lambda j: (0, j)),
                  pl.BlockSpec((1, bn), lambda j: (0, j))],
        out_specs=pl.BlockSpec((B, bn), lambda j: (0, j)),
        name="ada",
    )(c, w_ada, b_ada.reshape(1, n_out))


def _ffn_block(x, mod_ref, first_row, w_in_ref, w_out_ref, pg_ref, pb_ref):
    shift, scale, gate = (mod_ref[0, first_row + r:first_row + r + 1, :] for r in range(3))
    h = (x * (1.0 + scale) + shift).astype(bf16)
    y = _swiglu(h, w_in_ref, w_out_ref)
    return _layernorm(ALPHA * x + 0.5 * gate * y, pg_ref[...], pb_ref[...])


def _ffn1_kernel(x_ref, mod_ref, lng_ref, lnb_ref, w_in_ref, w_out_ref, pg_ref, pb_ref, o_ref):
    for r0 in range(0, x_ref.shape[1], TM_SUB):
        rows = slice(r0, r0 + TM_SUB)
        x = _layernorm(x_ref[0, rows, :], lng_ref[...], lnb_ref[...])
        o_ref[0, rows, :] = _ffn_block(x, mod_ref, 0, w_in_ref, w_out_ref, pg_ref, pb_ref)


def _ffn2_kernel(x_ref, attn_ref, yp_ref, mod_ref, woa_ref, wop_ref, pg1_ref, pb1_ref,
                 w_in_ref, w_out_ref, pg2_ref, pb2_ref, o_ref):
    gate1 = mod_ref[0, 5:6, :]

    def mixer_residual(rows):
        y = (jnp.dot(attn_ref[0, rows, :], woa_ref[...], preferred_element_type=f32)
             + jnp.dot(yp_ref[0, rows, :], wop_ref[...], preferred_element_type=f32))
        return _layernorm(ALPHA * x_ref[0, rows, :] + gate1 * y, pg1_ref[...], pb1_ref[...])

    subs = [slice(r0, r0 + TM_SUB) for r0 in range(0, x_ref.shape[1], TM_SUB)]
    x_next = mixer_residual(subs[0])
    for n, rows in enumerate(subs):
        x = x_next
        if n + 1 < len(subs):
            x_next = mixer_residual(subs[n + 1])
        o_ref[0, rows, :] = _ffn_block(x, mod_ref, 6, w_in_ref, w_out_ref, pg2_ref, pb2_ref)


def _resident(shape):
    return pl.BlockSpec(shape, lambda b, i: (0,) * len(shape), pipeline_mode=pl.Buffered(1))


def _row_spec(tm, width):
    return pl.BlockSpec((1, tm, width), lambda b, i: (b, i, 0))


def _mod_spec():
    return pl.BlockSpec((1, 9, D_MODEL), lambda b, i: (b, 0, 0))


def _ffn1(x, mod, lng, lnb, w_in, w_out, pg, pb):
    B, S, D = x.shape
    vec = _resident((1, D))
    return pl.pallas_call(
        _ffn1_kernel,
        out_shape=jax.ShapeDtypeStruct((B, S, D), f32),
        grid=(B, S // TM_FFN),
        in_specs=[_row_spec(TM_FFN, D), _mod_spec(), vec, vec,
                  _resident(w_in.shape), _resident(w_out.shape), vec, vec],
        out_specs=_row_spec(TM_FFN, D),
        compiler_params=pltpu.CompilerParams(
            dimension_semantics=("parallel", "parallel"), vmem_limit_bytes=VMEM_LIMIT),
        name="ffn1",
    )(x, mod, lng, lnb, w_in, w_out, pg, pb)


def _ffn2(x1, attn, yp, mod, woa, wop, pg1, pb1, w_in, w_out, pg2, pb2):
    B, S, D = x1.shape
    vec = _resident((1, D))
    return pl.pallas_call(
        _ffn2_kernel,
        out_shape=jax.ShapeDtypeStruct((B, S, D), f32),
        grid=(B, S // TM_FFN),
        in_specs=[_row_spec(TM_FFN, D), _row_spec(TM_FFN, attn.shape[-1]),
                  _row_spec(TM_FFN, yp.shape[-1]), _mod_spec(),
                  _resident(woa.shape), _resident(wop.shape), vec, vec,
                  _resident(w_in.shape), _resident(w_out.shape), vec, vec],
        out_specs=_row_spec(TM_FFN, D),
        compiler_params=pltpu.CompilerParams(
            dimension_semantics=("parallel", "parallel"), vmem_limit_bytes=VMEM_LIMIT),
        name="ffn2",
    )(x1, attn, yp, mod, woa, wop, pg1, pb1, w_in, w_out, pg2, pb2)


def _apply_rope(t, cos_t, sin_t):
    return t * cos_t + pltpu.roll(t, HEAD_PAD - QK_ROPE, 1) * sin_t


def _rope_slot_tables(pos_row, freq_col):
    t = pos_row.shape[1]
    ang = freq_col * pos_row
    c, s = jnp.cos(ang), jnp.sin(ang)
    zeros = jnp.zeros((QK_ROPE, t), f32)
    cos_rows = jnp.concatenate([jnp.ones((QK_NOPE, t), f32), c, c, zeros], axis=0)
    sin_rows = jnp.concatenate([jnp.zeros((QK_NOPE, t), f32), s, s, zeros], axis=0)
    return cos_rows, sin_rows


def _mix_kernel(x_ref, mod_ref, pos_ref, freq_ref, w_in_ref, qg_ref, kvg_ref, wuqt_ref, wuk_ref,
                wuvt_ref, wpool_ref, pbias_ref, pscale_ref,
                qt_ref, k_ref, vt_ref, yp_ref, tail_ref):
    st = pl.program_id(1)
    tm = x_ref.shape[1]
    shift, scale = mod_ref[0, 3:4, :], mod_ref[0, 4:5, :]
    ones_rows = (lax.broadcasted_iota(jnp.int32, (V_SLAB - V_DIM, TK), 0) == 0).astype(bf16)

    @pl.when(st == 0)
    def _():
        tail_ref[...] = jnp.zeros_like(tail_ref)

    tail = tail_ref[...]
    for t in range(tm // TK):
        rows = slice(t * TK, (t + 1) * TK)
        h = (x_ref[0, rows, :] * (1.0 + scale) + shift).astype(bf16)
        proj = jnp.dot(h, w_in_ref[...], preferred_element_type=f32)
        cq = proj[:, :Q_LORA]
        ckv = proj[:, Q_LORA:Q_LORA + KV_LORA]
        krb = proj[:, Q_LORA + KV_LORA:Q_LORA + KV_LORA + HEAD_PAD]
        u = proj[:, Q_LORA + KV_LORA + HEAD_PAD:]
        cos_rows, sin_rows = _rope_slot_tables(pos_ref[0, :, rows].astype(f32), freq_ref[...])
        cos_t, sin_t = cos_rows.T, sin_rows.T

        cqn = _rmsnorm(cq, qg_ref[...]).astype(bf16)
        q_t = lax.dot_general(wuqt_ref[...], cqn, _NT, preferred_element_type=f32)
        for hh in range(N_HEADS):
            sl = slice(hh * HEAD_PAD, (hh + 1) * HEAD_PAD)
            blk = q_t[sl, :]
            blk = blk * cos_rows + pltpu.roll(blk, HEAD_PAD - QK_ROPE, 0) * sin_rows
            qt_ref[0, t, sl, :] = (blk * (SM_SCALE * LOG2_E)).astype(bf16)
        ckvn = _rmsnorm(ckv, kvg_ref[...]).astype(bf16)
        k_all = jnp.dot(ckvn, wuk_ref[...], preferred_element_type=f32)
        v_t = lax.dot_general(wuvt_ref[...], ckvn, _NT, preferred_element_type=f32).astype(bf16)
        for hh in range(N_HEADS):
            vt_ref[0, t, hh * V_SLAB:hh * V_SLAB + V_DIM, :] = v_t[hh * V_DIM:(hh + 1) * V_DIM, :]
            vt_ref[0, t, hh * V_SLAB + V_DIM:(hh + 1) * V_SLAB, :] = ones_rows
        k_rope = _apply_rope(krb, cos_t, sin_t)
        for hh in range(N_HEADS):
            sl = slice(hh * HEAD_PAD, (hh + 1) * HEAD_PAD)
            k_ref[0, rows, sl] = (k_all[:, sl] + k_rope).astype(bf16)

        ext = jnp.concatenate([tail, u], axis=0)
        tail = u[TK - POOL_HALO:, :]
        t_idx = st * tm + t * TK + lax.broadcasted_iota(jnp.int32, (TK, 1), 0)
        groups = []
        for gi, w in enumerate(POOL_WINDOWS):
            e = ext[:, gi * POOL_GROUP:(gi + 1) * POOL_GROUP]
            sh = 1
            while sh < w:
                e = e + pltpu.roll(e, sh, 0)
                sh *= 2
            cnt = jnp.minimum(t_idx + 1, w).astype(f32)
            groups.append(e[POOL_HALO:, :] / cnt - u[:, gi * POOL_GROUP:(gi + 1) * POOL_GROUP])
        pooled = jnp.concatenate(groups, axis=1).astype(bf16)
        yp = jnp.dot(pooled, wpool_ref[...], preferred_element_type=f32) + pbias_ref[...]
        yp_ref[0, rows, :] = (yp * pscale_ref[...]).astype(bf16)
    tail_ref[...] = tail


def _mix(x1, mod, pos, freq, w_in, qg, kvg, wuq, wuk, wuvt, wpool, pbias, pscale):
    B, S, D = x1.shape
    assert TQ == TK
    tm = TM_MIX
    hw = N_HEADS * HEAD_PAD
    vw = N_HEADS * V_SLAB

    def const(shape):
        return pl.BlockSpec(shape, lambda b, i: (0,) * len(shape))

    return pl.pallas_call(
        _mix_kernel,
        out_shape=(jax.ShapeDtypeStruct((B, S // TQ, hw, TQ), bf16),
                   jax.ShapeDtypeStruct((B, S, hw), bf16),
                   jax.ShapeDtypeStruct((B, S // TK, vw, TK), bf16),
                   jax.ShapeDtypeStruct((B, S, POOL_WIDTH), bf16)),
        grid=(B, S // tm),
        in_specs=[_row_spec(tm, D), _mod_spec(),
                  pl.BlockSpec((1, 1, tm), lambda b, i: (b, 0, i)), const(freq.shape),
                  const(w_in.shape), const(qg.shape), const(kvg.shape), const(wuq.shape),
                  const(wuk.shape), const(wuvt.shape), const(wpool.shape), const(pbias.shape),
                  const(pscale.shape)],
        out_specs=(pl.BlockSpec((1, tm // TQ, hw, TQ), lambda b, i: (b, i, 0, 0)),
                   _row_spec(tm, hw),
                   pl.BlockSpec((1, tm // TK, vw, TK), lambda b, i: (b, i, 0, 0)),
                   _row_spec(tm, POOL_WIDTH)),
        scratch_shapes=[pltpu.VMEM((POOL_HALO, POOL_WIDTH), f32)],
        compiler_params=pltpu.CompilerParams(
            dimension_semantics=("parallel", "arbitrary"), vmem_limit_bytes=VMEM_LIMIT),
        name="mix_proj",
    )(x1, mod, pos, freq, w_in, qg, kvg, wuq, wuk, wuvt, wpool, pbias, pscale)


def _attn_kernel(qt_ref, k_ref, vt_ref, o_ref, *scratch):
    for sub in range(Q_PER_STEP):
        _attn_q_tile(pl.program_id(1) * Q_PER_STEP + sub, qt_ref.at[0, sub], k_ref, vt_ref,
                     o_ref.at[0, sub * TQ:(sub + 1) * TQ, :], *scratch)


def _attn_q_tile(i, qt_ref, k_ref, vt_ref, o_ref, m_ref, acc_ref, s_a, s_b, mt_a, mt_b):
    m_ref[...] = jnp.full_like(m_ref, -jnp.inf)
    acc_ref[...] = jnp.zeros_like(acc_ref)

    def qk_head(hh, j, s_buf, mt_buf, diag):
        koff = pl.multiple_of(j * TK, TK)
        sl = slice(hh * HEAD_PAD, (hh + 1) * HEAD_PAD)
        s = jnp.dot(k_ref[0, pl.ds(koff, TK), sl], qt_ref[sl, :],
                    preferred_element_type=f32)
        if diag is not None:
            key_chunk = lax.broadcasted_iota(jnp.int32, (TK, TQ), 0) // CHUNK + diag * (TK // CHUNK)
            qry_chunk = lax.broadcasted_iota(jnp.int32, (TK, TQ), 1) // CHUNK
            s = jnp.where(key_chunk <= qry_chunk, s, -jnp.inf)
        s_buf[hh] = s
        mt_buf[hh] = jnp.max(s, axis=0, keepdims=True)

    def pv_head(hh, j, s_buf, mt_buf):
        m_prev = m_ref[hh]
        m_new = jnp.maximum(m_prev, mt_buf[hh])
        alpha = jnp.exp2(m_prev - m_new)
        p = jnp.exp2(s_buf[hh] - m_new).astype(bf16)
        v_t = vt_ref[0, j, hh * V_SLAB:(hh + 1) * V_SLAB, :]
        acc_ref[hh] = alpha * acc_ref[hh] + jnp.dot(v_t, p, preferred_element_type=f32)
        m_ref[hh] = m_new

    def stage(qk=None, pv=None):
        for hh in range(N_HEADS):
            if qk is not None:
                qk_head(hh, *qk)
            if pv is not None:
                pv_head(hh, *pv)

    stage(qk=(i, s_a, mt_a, 0))
    n_pairs = i // 2

    def pair(u):
        stage(qk=(2 * u, s_b, mt_b, None), pv=(jnp.where(u == 0, i, 2 * u - 1), s_a, mt_a))
        stage(qk=(2 * u + 1, s_a, mt_a, None), pv=(2 * u, s_b, mt_b))

    def body(v, carry):
        pair(2 * v)
        pair(2 * v + 1)
        return carry

    lax.fori_loop(0, n_pairs // 2, body, 0)

    @pl.when(n_pairs % 2 == 1)
    def _():
        pair(n_pairs - 1)

    tile_a = jnp.where(n_pairs == 0, i, 2 * n_pairs - 1)

    @pl.when(i % 2 == 1)
    def _():
        stage(qk=(i - 1, s_b, mt_b, None), pv=(tile_a, s_a, mt_a))
        stage(pv=(i - 1, s_b, mt_b))

    @pl.when(i % 2 == 0)
    def _():
        stage(pv=(tile_a, s_a, mt_a))

    o_t = jnp.concatenate([acc_ref[hh, :V_DIM, :] / acc_ref[hh, V_DIM:V_DIM + 1, :]
                           for hh in range(N_HEADS)], axis=0)
    o_ref[...] = o_t.T.astype(bf16)


def _attention(q_t, k, v_t):
    B, S, hw = k.shape
    vw = N_HEADS * V_DIM
    return pl.pallas_call(
        _attn_kernel,
        out_shape=jax.ShapeDtypeStruct((B, S, vw), bf16),
        grid=(B, S // (TQ * Q_PER_STEP)),
        in_specs=[pl.BlockSpec((1, Q_PER_STEP, hw, TQ), lambda b, i: (b, i, 0, 0)),
                  pl.BlockSpec((1, S, hw), lambda b, i: (b, 0, 0)),
                  pl.BlockSpec((1,) + v_t.shape[1:], lambda b, i: (b, 0, 0, 0))],
        out_specs=_row_spec(TQ * Q_PER_STEP, vw),
        scratch_shapes=[pltpu.VMEM((N_HEADS, 1, TQ), f32),
                        pltpu.VMEM((N_HEADS, V_SLAB, TQ), f32),
                        pltpu.VMEM((N_HEADS, TK, TQ), f32),
                        pltpu.VMEM((N_HEADS, TK, TQ), f32),
                        pltpu.VMEM((N_HEADS, 1, TQ), f32),
                        pltpu.VMEM((N_HEADS, 1, TQ), f32)],
        compiler_params=pltpu.CompilerParams(
            dimension_semantics=("parallel", "parallel"), vmem_limit_bytes=VMEM_LIMIT),
        name="attention",
    )(q_t, k, v_t)


def _rot_cols(w):
    half = w.shape[-1] // 2
    return jnp.concatenate([-w[..., half:], w[..., :half]], axis=-1)


def _head_slots(parts):
    k_dim = parts[0][0].shape[0]
    cols = []
    for blocks in parts:
        used = sum(b.shape[1] for b in blocks)
        cols.extend(blocks)
        if used < HEAD_PAD:
            cols.append(jnp.zeros((k_dim, HEAD_PAD - used), f32))
    return jnp.concatenate(cols, axis=1)


def kernel(x, c, positions, ln_in_g, ln_in_b, w_ada, b_ada, ffn1_w_in, ffn1_w_out, w_in, q_norm_g, kv_norm_g, w_uq, w_ukv, pool_w, pool_b, pool_scale, w_out, ffn2_w_in, ffn2_w_out, post_ln_g, post_ln_b):
    B, S, D = x.shape
    l = 0
    row = lambda a: a.reshape(1, -1)

    qh = QK_NOPE + QK_ROPE
    wuq = w_uq[l]
    wuq_pad = _head_slots([[wuq[:, h * qh:h * qh + QK_NOPE],
                            wuq[:, h * qh + QK_NOPE:(h + 1) * qh],
                            _rot_cols(wuq[:, h * qh + QK_NOPE:(h + 1) * qh])]
                           for h in range(N_HEADS)]).astype(bf16)
    kvh = QK_NOPE + V_DIM
    wukv = w_ukv[l]
    wuk_pad = _head_slots([[wukv[:, h * kvh:h * kvh + QK_NOPE]] for h in range(N_HEADS)]).astype(bf16)
    wuv_t = jnp.concatenate([wukv[:, h * kvh + QK_NOPE:(h + 1) * kvh] for h in range(N_HEADS)],
                            axis=1).T.astype(bf16)
    w_kr = w_in[l][:, Q_LORA + KV_LORA:Q_LORA + KV_LORA + QK_ROPE]
    w_in_pad = jnp.concatenate(
        [w_in[l][:, :Q_LORA + KV_LORA], jnp.zeros((D, QK_NOPE), f32), w_kr, _rot_cols(w_kr),
         w_in[l][:, Q_LORA + KV_LORA + QK_ROPE:]], axis=1).astype(bf16)
    wpool = jax.scipy.linalg.block_diag(*[pool_w[l, g] for g in range(len(POOL_WINDOWS))]).astype(bf16)
    wo = w_out[l]
    woa = wo[:N_HEADS * V_DIM].astype(bf16)
    wop = wo[N_HEADS * V_DIM:].astype(bf16)

    inv_freq = ROPE_THETA ** (-jnp.arange(0, QK_ROPE, 2, dtype=f32) / QK_ROPE)

    mod = _ada(c, w_ada[l], b_ada[l]).reshape(B, 9, D)

    x1 = _ffn1(x, mod, row(ln_in_g), row(ln_in_b), ffn1_w_in[l].astype(bf16),
               ffn1_w_out[l].astype(bf16), row(post_ln_g[l, 0]), row(post_ln_b[l, 0]))
    q_t, k, v_t, yp = _mix(x1, mod, positions.reshape(B, 1, S), inv_freq.reshape(-1, 1), w_in_pad,
                           row(q_norm_g[l]), row(kv_norm_g[l]),
                           wuq_pad.T, wuk_pad, wuv_t, wpool, row(pool_b[l]), row(pool_scale[l]))
    attn = _attention(q_t, k, v_t)
    return _ffn2(x1, attn, yp, mod, woa, wop, row(post_ln_g[l, 1]), row(post_ln_b[l, 1]),
                 ffn2_w_in[l].astype(bf16), ffn2_w_out[l].astype(bf16),
                 row(post_ln_g[l, 2]), row(post_ln_b[l, 2]))
```

```python
import jax
import jax.numpy as jnp
from jax import lax
from jax.experimental import pallas as pl
from jax.experimental.pallas import tpu as pltpu

f32 = jnp.float32
bf16 = jnp.bfloat16

D_MODEL = 1024
D_FF = 2816
N_HEADS = 8
QK_NOPE = 64
QK_ROPE = 32
V_DIM = 64
Q_LORA = 512
KV_LORA = 256
HEAD_PAD = 128
POOL_WINDOWS = (2, 4, 8, 16)
POOL_GROUP = 128
POOL_WIDTH = 512
POOL_HALO = 16
CHUNK = 64
ROPE_THETA = 10000.0
LN_EPS = 1e-5
RMS_EPS = 1e-6
ALPHA = 2.0 ** 0.25
SM_SCALE = (QK_NOPE + QK_ROPE) ** -0.5
LOG2_E = 1.4426950408889634
V_SLAB = 80
IN_PAD = Q_LORA + KV_LORA + HEAD_PAD + POOL_WIDTH

TM_FFN = 1024
TM_SUB = 256
TM_MIX = 1024
TQ = 256
TK = 256
Q_PER_STEP = 2
FF_CHUNKS = (1024, 1024, 768)
VMEM_LIMIT = 56 * 1024 * 1024

_NT = (((1,), (1,)), ((), ()))


def _layernorm(x, g, b):
    mu = jnp.mean(x, axis=-1, keepdims=True)
    xc = x - mu
    var = jnp.mean(xc * xc, axis=-1, keepdims=True)
    return xc * lax.rsqrt(var + LN_EPS) * g + b


def _rmsnorm(x, g):
    ms = jnp.mean(x * x, axis=-1, keepdims=True)
    return x * lax.rsqrt(ms + RMS_EPS) * g


def _swiglu(h, w_in_ref, w_out_ref):
    y = None
    off = 0
    for n in FF_CHUNKS:
        g = jnp.dot(h, w_in_ref[:, off:off + n], preferred_element_type=f32)
        u = jnp.dot(h, w_in_ref[:, D_FF + off:D_FF + off + n], preferred_element_type=f32)
        a = (g / (1.0 + jnp.exp(-g)) * u).astype(bf16)
        part = jnp.dot(a, w_out_ref[off:off + n, :], preferred_element_type=f32)
        y = part if y is None else y + part
        off += n
    return y


def _ada_kernel(c_ref, w_ref, b_ref, o_ref):
    c = c_ref[...]
    c_act = (c / (1.0 + jnp.exp(-c))).astype(bf16)
    o_ref[...] = jnp.dot(c_act, w_ref[...].astype(bf16), preferred_element_type=f32) + b_ref[...]


def _ada(c, w_ada, b_ada):
    B = c.shape[0]
    n_out = w_ada.shape[1]
    bn = D_MODEL
    return pl.pallas_call(
        _ada_kernel,
        out_shape=jax.ShapeDtypeStruct((B, n_out), f32),
        grid=(n_out // bn,),
        in_specs=[pl.BlockSpec((B, D_MODEL), lambda j: (0, 0)),
                  pl.BlockSpec((D_MODEL, bn), lambda j: (0, j)),
                  pl.BlockSpec((1, bn), lambda j: (0, j))],
        out_specs=pl.BlockSpec((B, bn), lambda j: (0, j)),
        name="ada",
    )(c, w_ada, b_ada.reshape(1, n_out))


def _ffn_block(x, mod_ref, first_row, w_in_ref, w_out_ref, pg_ref, pb_ref):
    shift, scale, gate = (mod_ref[0, first_row + r:first_row + r + 1, :] for r in range(3))
    h = (x * (1.0 + scale) + shift).astype(bf16)
    y = _swiglu(h, w_in_ref, w_out_ref)
    return _layernorm(ALPHA * x + 0.5 * gate * y, pg_ref[...], pb_ref[...])


def _ffn1_kernel(x_ref, mod_ref, lng_ref, lnb_ref, w_in_ref, w_out_ref, pg_ref, pb_ref, o_ref):
    for r0 in range(0, x_ref.shape[1], TM_SUB):
        rows = slice(r0, r0 + TM_SUB)
        x = _layernorm(x_ref[0, rows, :], lng_ref[...], lnb_ref[...])
        o_ref[0, rows, :] = _ffn_block(x, mod_ref, 0, w_in_ref, w_out_ref, pg_ref, pb_ref)


def _ffn2_kernel(x_ref, attn_ref, yp_ref, mod_ref, woa_ref, wop_ref, pg1_ref, pb1_ref,
                 w_in_ref, w_out_ref, pg2_ref, pb2_ref, o_ref):
    gate1 = mod_ref[0, 5:6, :]

    def mixer_residual(rows):
        y = (jnp.dot(attn_ref[0, rows, :], woa_ref[...], preferred_element_type=f32)
             + jnp.dot(yp_ref[0, rows, :], wop_ref[...], preferred_element_type=f32))
        return _layernorm(ALPHA * x_ref[0, rows, :] + gate1 * y, pg1_ref[...], pb1_ref[...])

    subs = [slice(r0, r0 + TM_SUB) for r0 in range(0, x_ref.shape[1], TM_SUB)]
    x_next = mixer_residual(subs[0])
    for n, rows in enumerate(subs):
        x = x_next
        if n + 1 < len(subs):
            x_next = mixer_residual(subs[n + 1])
        o_ref[0, rows, :] = _ffn_block(x, mod_ref, 6, w_in_ref, w_out_ref, pg2_ref, pb2_ref)


def _resident(shape):
    return pl.BlockSpec(shape, lambda b, i: (0,) * len(shape), pipeline_mode=pl.Buffered(1))


def _row_spec(tm, width):
    return pl.BlockSpec((1, tm, width), lambda b, i: (b, i, 0))


def _mod_spec():
    return pl.BlockSpec((1, 9, D_MODEL), lambda b, i: (b, 0, 0))


def _ffn1(x, mod, lng, lnb, w_in, w_out, pg, pb):
    B, S, D = x.shape
    vec = _resident((1, D))
    return pl.pallas_call(
        _ffn1_kernel,
        out_shape=jax.ShapeDtypeStruct((B, S, D), f32),
        grid=(B, S // TM_FFN),
        in_specs=[_row_spec(TM_FFN, D), _mod_spec(), vec, vec,
                  _resident(w_in.shape), _resident(w_out.shape), vec, vec],
        out_specs=_row_spec(TM_FFN, D),
        compiler_params=pltpu.CompilerParams(
            dimension_semantics=("parallel", "parallel"), vmem_limit_bytes=VMEM_LIMIT),
        name="ffn1",
    )(x, mod, lng, lnb, w_in, w_out, pg, pb)


def _ffn2(x1, attn, yp, mod, woa, wop, pg1, pb1, w_in, w_out, pg2, pb2):
    B, S, D = x1.shape
    vec = _resident((1, D))
    return pl.pallas_call(
        _ffn2_kernel,
        out_shape=jax.ShapeDtypeStruct((B, S, D), f32),
        grid=(B, S // TM_FFN),
        in_specs=[_row_spec(TM_FFN, D), _row_spec(TM_FFN, attn.shape[-1]),
                  _row_spec(TM_FFN, yp.shape[-1]), _mod_spec(),
                  _resident(woa.shape), _resident(wop.shape), vec, vec,
                  _resident(w_in.shape), _resident(w_out.shape), vec, vec],
        out_specs=_row_spec(TM_FFN, D),
        compiler_params=pltpu.CompilerParams(
            dimension_semantics=("parallel", "parallel"), vmem_limit_bytes=VMEM_LIMIT),
        name="ffn2",
    )(x1, attn, yp, mod, woa, wop, pg1, pb1, w_in, w_out, pg2, pb2)


def _apply_rope(t, cos_t, sin_t):
    return t * cos_t + pltpu.roll(t, HEAD_PAD - QK_ROPE, 1) * sin_t


def _rope_slot_tables(pos_row, freq_col):
    t = pos_row.shape[1]
    ang = freq_col * pos_row
    c, s = jnp.cos(ang), jnp.sin(ang)
    zeros = jnp.zeros((QK_ROPE, t), f32)
    cos_rows = jnp.concatenate([jnp.ones((QK_NOPE, t), f32), c, c, zeros], axis=0)
    sin_rows = jnp.concatenate([jnp.zeros((QK_NOPE, t), f32), s, s, zeros], axis=0)
    return cos_rows, sin_rows


def _mix_kernel(x_ref, mod_ref, pos_ref, freq_ref, w_in_ref, qg_ref, kvg_ref, wuqt_ref, wuk_ref,
                wuvt_ref, wpool_ref, pbias_ref, pscale_ref,
                qt_ref, k_ref, vt_ref, yp_ref, tail_ref):
    st = pl.program_id(1)
    tm = x_ref.shape[1]
    shift, scale = mod_ref[0, 3:4, :], mod_ref[0, 4:5, :]
    ones_rows = (lax.broadcasted_iota(jnp.int32, (V_SLAB - V_DIM, TK), 0) == 0).astype(bf16)

    @pl.when(st == 0)
    def _():
        tail_ref[...] = jnp.zeros_like(tail_ref)

    tail = tail_ref[...]
    for t in range(tm // TK):
        rows = slice(t * TK, (t + 1) * TK)
        h = (x_ref[0, rows, :] * (1.0 + scale) + shift).astype(bf16)
        proj = jnp.dot(h, w_in_ref[...], preferred_element_type=f32)
        cq = proj[:, :Q_LORA]
        ckv = proj[:, Q_LORA:Q_LORA + KV_LORA]
        krb = proj[:, Q_LORA + KV_LORA:Q_LORA + KV_LORA + HEAD_PAD]
        u = proj[:, Q_LORA + KV_LORA + HEAD_PAD:]
        cos_rows, sin_rows = _rope_slot_tables(pos_ref[0, :, rows].astype(f32), freq_ref[...])
        cos_t, sin_t = cos_rows.T, sin_rows.T

        cqn = _rmsnorm(cq, qg_ref[...]).astype(bf16)
        q_t = lax.dot_general(wuqt_ref[...], cqn, _NT, preferred_element_type=f32)
        for hh in range(N_HEADS):
            sl = slice(hh * HEAD_PAD, (hh + 1) * HEAD_PAD)
            blk = q_t[sl, :]
            blk = blk * cos_rows + pltpu.roll(blk, HEAD_PAD - QK_ROPE, 0) * sin_rows
            qt_ref[0, t, sl, :] = (blk * (SM_SCALE * LOG2_E)).astype(bf16)
        ckvn = _rmsnorm(ckv, kvg_ref[...]).astype(bf16)
        k_all = jnp.dot(ckvn, wuk_ref[...], preferred_element_type=f32)
        v_t = lax.dot_general(wuvt_ref[...], ckvn, _NT, preferred_element_type=f32).astype(bf16)
        for hh in range(N_HEADS):
            vt_ref[0, t, hh * V_SLAB:hh * V_SLAB + V_DIM, :] = v_t[hh * V_DIM:(hh + 1) * V_DIM, :]
            vt_ref[0, t, hh * V_SLAB + V_DIM:(hh + 1) * V_SLAB, :] = ones_rows
        k_rope = _apply_rope(krb, cos_t, sin_t)
        for hh in range(N_HEADS):
            sl = slice(hh * HEAD_PAD, (hh + 1) * HEAD_PAD)
            k_ref[0, rows, sl] = (k_all[:, sl] + k_rope).astype(bf16)

        ext = jnp.concatenate([tail, u], axis=0)
        tail = u[TK - POOL_HALO:, :]
        t_idx = st * tm + t * TK + lax.broadcasted_iota(jnp.int32, (TK, 1), 0)
        groups = []
        for gi, w in enumerate(POOL_WINDOWS):
            e = ext[:, gi * POOL_GROUP:(gi + 1) * POOL_GROUP]
            sh = 1
            while sh < w:
                e = e + pltpu.roll(e, sh, 0)
                sh *= 2
            cnt = jnp.minimum(t_idx + 1, w).astype(f32)
            groups.append(e[POOL_HALO:, :] / cnt - u[:, gi * POOL_GROUP:(gi + 1) * POOL_GROUP])
        pooled = jnp.concatenate(groups, axis=1).astype(bf16)
        yp = jnp.dot(pooled, wpool_ref[...], preferred_element_type=f32) + pbias_ref[...]
        yp_ref[0, rows, :] = (yp * pscale_ref[...]).astype(bf16)
    tail_ref[...] = tail


def _mix(x1, mod, pos, freq, w_in, qg, kvg, wuq, wuk, wuvt, wpool, pbias, pscale):
    B, S, D = x1.shape
    assert TQ == TK
    tm = TM_MIX
    hw = N_HEADS * HEAD_PAD
    vw = N_HEADS * V_SLAB

    def const(shape):
        return pl.BlockSpec(shape, lambda b, i: (0,) * len(shape))

    return pl.pallas_call(
        _mix_kernel,
        out_shape=(jax.ShapeDtypeStruct((B, S // TQ, hw, TQ), bf16),
                   jax.ShapeDtypeStruct((B, S, hw), bf16),
                   jax.ShapeDtypeStruct((B, S // TK, vw, TK), bf16),
                   jax.ShapeDtypeStruct((B, S, POOL_WIDTH), bf16)),
        grid=(B, S // tm),
        in_specs=[_row_spec(tm, D), _mod_spec(),
                  pl.BlockSpec((1, 1, tm), lambda b, i: (b, 0, i)), const(freq.shape),
                  const(w_in.shape), const(qg.shape), const(kvg.shape), const(wuq.shape),
                  const(wuk.shape), const(wuvt.shape), const(wpool.shape), const(pbias.shape),
                  const(pscale.shape)],
        out_specs=(pl.BlockSpec((1, tm // TQ, hw, TQ), lambda b, i: (b, i, 0, 0)),
                   _row_spec(tm, hw),
                   pl.BlockSpec((1, tm // TK, vw, TK), lambda b, i: (b, i, 0, 0)),
                   _row_spec(tm, POOL_WIDTH)),
        scratch_shapes=[pltpu.VMEM((POOL_HALO, POOL_WIDTH), f32)],
        compiler_params=pltpu.CompilerParams(
            dimension_semantics=("parallel", "arbitrary"), vmem_limit_bytes=VMEM_LIMIT),
        name="mix_proj",
    )(x1, mod, pos, freq, w_in, qg, kvg, wuq, wuk, wuvt, wpool, pbias, pscale)


def _attn_kernel(qt_ref, k_ref, vt_ref, o_ref, *scratch):
    for sub in range(Q_PER_STEP):
        _attn_q_tile(pl.program_id(1) * Q_PER_STEP + sub, qt_ref.at[0, sub], k_ref, vt_ref,
                     o_ref.at[0, sub * TQ:(sub + 1) * TQ, :], *scratch)


def _attn_q_tile(i, qt_ref, k_ref, vt_ref, o_ref, m_ref, acc_ref, s_a, s_b, mt_a, mt_b):
    m_ref[...] = jnp.full_like(m_ref, -jnp.inf)
    acc_ref[...] = jnp.zeros_like(acc_ref)

    def qk_head(hh, j, s_buf, mt_buf, diag):
        koff = pl.multiple_of(j * TK, TK)
        sl = slice(hh * HEAD_PAD, (hh + 1) * HEAD_PAD)
        s = jnp.dot(k_ref[0, pl.ds(koff, TK), sl], qt_ref[sl, :],
                    preferred_element_type=f32)
        if diag is not None:
            key_chunk = lax.broadcasted_iota(jnp.int32, (TK, TQ), 0) // CHUNK + diag * (TK // CHUNK)
            qry_chunk = lax.broadcasted_iota(jnp.int32, (TK, TQ), 1) // CHUNK
            s = jnp.where(key_chunk <= qry_chunk, s, -jnp.inf)
        s_buf[hh] = s
        mt_buf[hh] = jnp.max(s, axis=0, keepdims=True)

    def pv_head(hh, j, s_buf, mt_buf):
        m_prev = m_ref[hh]
        m_new = jnp.maximum(m_prev, mt_buf[hh])
        alpha = jnp.exp2(m_prev - m_new)
        p = jnp.exp2((s_buf[hh] - m_new).astype(bf16))
        v_t = vt_ref[0, j, hh * V_SLAB:(hh + 1) * V_SLAB, :]
        acc_ref[hh] = alpha * acc_ref[hh] + jnp.dot(v_t, p, preferred_element_type=f32)
        m_ref[hh] = m_new

    def stage(qk=None, pv=None):
        for hh in range(N_HEADS):
            if qk is not None:
                qk_head(hh, *qk)
            if pv is not None:
                pv_head(hh, *pv)

    stage(qk=(i, s_a, mt_a, 0))
    n_pairs = i // 2

    def pair(u):
        stage(qk=(2 * u, s_b, mt_b, None), pv=(jnp.where(u == 0, i, 2 * u - 1), s_a, mt_a))
        stage(qk=(2 * u + 1, s_a, mt_a, None), pv=(2 * u, s_b, mt_b))

    def body(v, carry):
        pair(2 * v)
        pair(2 * v + 1)
        return carry

    lax.fori_loop(0, n_pairs // 2, body, 0)

    @pl.when(n_pairs % 2 == 1)
    def _():
        pair(n_pairs - 1)

    tile_a = jnp.where(n_pairs == 0, i, 2 * n_pairs - 1)

    @pl.when(i % 2 == 1)
    def _():
        stage(qk=(i - 1, s_b, mt_b, None), pv=(tile_a, s_a, mt_a))
        stage(pv=(i - 1, s_b, mt_b))

    @pl.when(i % 2 == 0)
    def _():
        stage(pv=(tile_a, s_a, mt_a))

    o_t = jnp.concatenate([acc_ref[hh, :V_DIM, :] / acc_ref[hh, V_DIM:V_DIM + 1, :]
                           for hh in range(N_HEADS)], axis=0)
    o_ref[...] = o_t.T.astype(bf16)


def _attention(q_t, k, v_t):
    B, S, hw = k.shape
    vw = N_HEADS * V_DIM
    return pl.pallas_call(
        _attn_kernel,
        out_shape=jax.ShapeDtypeStruct((B, S, vw), bf16),
        grid=(B, S // (TQ * Q_PER_STEP)),
        in_specs=[pl.BlockSpec((1, Q_PER_STEP, hw, TQ), lambda b, i: (b, i, 0, 0)),
                  pl.BlockSpec((1, S, hw), lambda b, i: (b, 0, 0)),
                  pl.BlockSpec((1,) + v_t.shape[1:], lambda b, i: (b, 0, 0, 0))],
        out_specs=_row_spec(TQ * Q_PER_STEP, vw),
        scratch_shapes=[pltpu.VMEM((N_HEADS, 1, TQ), f32),
                        pltpu.VMEM((N_HEADS, V_SLAB, TQ), f32),
                        pltpu.VMEM((N_HEADS, TK, TQ), f32),
                        pltpu.VMEM((N_HEADS, TK, TQ), f32),
                        pltpu.VMEM((N_HEADS, 1, TQ), f32),
                        pltpu.VMEM((N_HEADS, 1, TQ), f32)],
        compiler_params=pltpu.CompilerParams(
            dimension_semantics=("parallel", "parallel"), vmem_limit_bytes=VMEM_LIMIT),
        name="attention",
    )(q_t, k, v_t)


def _rot_cols(w):
    half = w.shape[-1] // 2
    return jnp.concatenate([-w[..., half:], w[..., :half]], axis=-1)


def _head_slots(parts):
    k_dim = parts[0][0].shape[0]
    cols = []
    for blocks in parts:
        used = sum(b.shape[1] for b in blocks)
        cols.extend(blocks)
        if used < HEAD_PAD:
            cols.append(jnp.zeros((k_dim, HEAD_PAD - used), f32))
    return jnp.concatenate(cols, axis=1)


def kernel(x, c, positions, ln_in_g, ln_in_b, w_ada, b_ada, ffn1_w_in, ffn1_w_out, w_in, q_norm_g, kv_norm_g, w_uq, w_ukv, pool_w, pool_b, pool_scale, w_out, ffn2_w_in, ffn2_w_out, post_ln_g, post_ln_b):
    B, S, D = x.shape
    l = 0
    row = lambda a: a.reshape(1, -1)

    qh = QK_NOPE + QK_ROPE
    wuq = w_uq[l]
    wuq_pad = _head_slots([[wuq[:, h * qh:h * qh + QK_NOPE],
                            wuq[:, h * qh + QK_NOPE:(h + 1) * qh],
                            _rot_cols(wuq[:, h * qh + QK_NOPE:(h + 1) * qh])]
                           for h in range(N_HEADS)]).astype(bf16)
    kvh = QK_NOPE + V_DIM
    wukv = w_ukv[l]
    wuk_pad = _head_slots([[wukv[:, h * kvh:h * kvh + QK_NOPE]] for h in range(N_HEADS)]).astype(bf16)
    wuv_t = jnp.concatenate([wukv[:, h * kvh + QK_NOPE:(h + 1) * kvh] for h in range(N_HEADS)],
                            axis=1).T.astype(bf16)
    w_kr = w_in[l][:, Q_LORA + KV_LORA:Q_LORA + KV_LORA + QK_ROPE]
    w_in_pad = jnp.concatenate(
        [w_in[l][:, :Q_LORA + KV_LORA], jnp.zeros((D, QK_NOPE), f32), w_kr, _rot_cols(w_kr),
         w_in[l][:, Q_LORA + KV_LORA + QK_ROPE:]], axis=1).astype(bf16)
    wpool = jax.scipy.linalg.block_diag(*[pool_w[l, g] for g in range(len(POOL_WINDOWS))]).astype(bf16)
    wo = w_out[l]
    woa = wo[:N_HEADS * V_DIM].astype(bf16)
    wop = wo[N_HEADS * V_DIM:].astype(bf16)

    inv_freq = ROPE_THETA ** (-jnp.arange(0, QK_ROPE, 2, dtype=f32) / QK_ROPE)

    mod = _ada(c, w_ada[l], b_ada[l]).reshape(B, 9, D)

    x1 = _ffn1(x, mod, row(ln_in_g), row(ln_in_b), ffn1_w_in[l].astype(bf16),
               ffn1_w_out[l].astype(bf16), row(post_ln_g[l, 0]), row(post_ln_b[l, 0]))
    q_t, k, v_t, yp = _mix(x1, mod, positions.reshape(B, 1, S), inv_freq.reshape(-1, 1), w_in_pad,
                           row(q_norm_g[l]), row(kv_norm_g[l]),
                           wuq_pad.T, wuk_pad, wuv_t, wpool, row(pool_b[l]), row(pool_scale[l]))
    attn = _attention(q_t, k, v_t)
    return _ffn2(x1, attn, yp, mod, woa, wop, row(post_ln_g[l, 1]), row(post_ln_b[l, 1]),
                 ffn2_w_in[l].astype(bf16), ffn2_w_out[l].astype(bf16),
                 row(post_ln_g[l, 2]), row(post_ln_b[l, 2]))
```

```python
import jax
import jax.numpy as jnp
from jax import lax
from jax.experimental import pallas as pl
from jax.experimental.pallas import tpu as pltpu

f32 = jnp.float32
bf16 = jnp.bfloat16

D_MODEL = 1024
D_FF = 2816
N_HEADS = 8
QK_NOPE = 64
QK_ROPE = 32
V_DIM = 64
Q_LORA = 512
KV_LORA = 256
HEAD_PAD = 128
POOL_WINDOWS = (2, 4, 8, 16)
POOL_GROUP = 128
POOL_WIDTH = 512
POOL_HALO = 16
CHUNK = 64
ROPE_THETA = 10000.0
LN_EPS = 1e-5
RMS_EPS = 1e-6
ALPHA = 2.0 ** 0.25
SM_SCALE = (QK_NOPE + QK_ROPE) ** -0.5
LOG2_E = 1.4426950408889634
V_SLAB = 80
IN_PAD = Q_LORA + KV_LORA + HEAD_PAD + POOL_WIDTH

TM_FFN = 1024
TM_SUB = 256
TM_MIX = 1024
TQ = 256
TK = 256
Q_PER_STEP = 2
FF_CHUNKS = (1024, 1024, 768)
VMEM_LIMIT = 56 * 1024 * 1024

_NT = (((1,), (1,)), ((), ()))


def _layernorm(x, g, b):
    mu = jnp.mean(x, axis=-1, keepdims=True)
    xc = x - mu
    var = jnp.mean(xc * xc, axis=-1, keepdims=True)
    return xc * lax.rsqrt(var + LN_EPS) * g + b


def _rmsnorm(x, g):
    ms = jnp.mean(x * x, axis=-1, keepdims=True)
    return x * lax.rsqrt(ms + RMS_EPS) * g


def _swiglu(h, w_in_ref, w_out_ref):
    y = None
    off = 0
    for n in FF_CHUNKS:
        g = jnp.dot(h, w_in_ref[:, off:off + n], preferred_element_type=f32)
        u = jnp.dot(h, w_in_ref[:, D_FF + off:D_FF + off + n], preferred_element_type=f32)
        a = (g / (1.0 + jnp.exp(-g)) * u).astype(bf16)
        part = jnp.dot(a, w_out_ref[off:off + n, :], preferred_element_type=f32)
        y = part if y is None else y + part
        off += n
    return y


def _ada_kernel(c_ref, w_ref, b_ref, o_ref):
    c = c_ref[...]
    c_act = (c / (1.0 + jnp.exp(-c))).astype(bf16)
    o_ref[...] = jnp.dot(c_act, w_ref[...].astype(bf16), preferred_element_type=f32) + b_ref[...]


def _ada(c, w_ada, b_ada):
    B = c.shape[0]
    n_out = w_ada.shape[1]
    bn = D_MODEL
    return pl.pallas_call(
        _ada_kernel,
        out_shape=jax.ShapeDtypeStruct((B, n_out), f32),
        grid=(n_out // bn,),
        in_specs=[pl.BlockSpec((B, D_MODEL), lambda j: (0, 0)),
                  pl.BlockSpec((D_MODEL, bn), lambda j: (0, j)),
                  pl.BlockSpec((1, bn), lambda j: (0, j))],
        out_specs=pl.BlockSpec((B, bn), lambda j: (0, j)),
        name="ada",
    )(c, w_ada, b_ada.reshape(1, n_out))


def _ffn_block(x, mod_ref, first_row, w_in_ref, w_out_ref, pg_ref, pb_ref):
    shift, scale, gate = (mod_ref[0, first_row + r:first_row + r + 1, :] for r in range(3))
    h = (x * (1.0 + scale) + shift).astype(bf16)
    y = _swiglu(h, w_in_ref, w_out_ref)
    return _layernorm(ALPHA * x + 0.5 * gate * y, pg_ref[...], pb_ref[...])


def _ffn1_kernel(x_ref, mod_ref, lng_ref, lnb_ref, w_in_ref, w_out_ref, pg_ref, pb_ref, o_ref):
    for r0 in range(0, x_ref.shape[1], TM_SUB):
        rows = slice(r0, r0 + TM_SUB)
        x = _layernorm(x_ref[0, rows, :], lng_ref[...], lnb_ref[...])
        o_ref[0, rows, :] = _ffn_block(x, mod_ref, 0, w_in_ref, w_out_ref, pg_ref, pb_ref)


def _ffn2_kernel(x_ref, attn_ref, yp_ref, mod_ref, woa_ref, wop_ref, pg1_ref, pb1_ref,
                 w_in_ref, w_out_ref, pg2_ref, pb2_ref, o_ref):
    gate1 = mod_ref[0, 5:6, :]

    def mixer_residual(rows):
        y = (jnp.dot(attn_ref[0, rows, :], woa_ref[...], preferred_element_type=f32)
             + jnp.dot(yp_ref[0, rows, :], wop_ref[...], preferred_element_type=f32))
        return _layernorm(ALPHA * x_ref[0, rows, :] + gate1 * y, pg1_ref[...], pb1_ref[...])

    subs = [slice(r0, r0 + TM_SUB) for r0 in range(0, x_ref.shape[1], TM_SUB)]
    x_next = mixer_residual(subs[0])
    for n, rows in enumerate(subs):
        x = x_next
        if n + 1 < len(subs):
            x_next = mixer_residual(subs[n + 1])
        o_ref[0, rows, :] = _ffn_block(x, mod_ref, 6, w_in_ref, w_out_ref, pg2_ref, pb2_ref)


def _resident(shape):
    return pl.BlockSpec(shape, lambda b, i: (0,) * len(shape), pipeline_mode=pl.Buffered(1))


def _row_spec(tm, width):
    return pl.BlockSpec((1, tm, width), lambda b, i: (b, i, 0))


def _mod_spec():
    return pl.BlockSpec((1, 9, D_MODEL), lambda b, i: (b, 0, 0))


def _ffn1(x, mod, lng, lnb, w_in, w_out, pg, pb):
    B, S, D = x.shape
    vec = _resident((1, D))
    return pl.pallas_call(
        _ffn1_kernel,
        out_shape=jax.ShapeDtypeStruct((B, S, D), f32),
        grid=(B, S // TM_FFN),
        in_specs=[_row_spec(TM_FFN, D), _mod_spec(), vec, vec,
                  _resident(w_in.shape), _resident(w_out.shape), vec, vec],
        out_specs=_row_spec(TM_FFN, D),
        compiler_params=pltpu.CompilerParams(
            dimension_semantics=("parallel", "parallel"), vmem_limit_bytes=VMEM_LIMIT),
        name="ffn1",
    )(x, mod, lng, lnb, w_in, w_out, pg, pb)


def _ffn2(x1, attn, yp, mod, woa, wop, pg1, pb1, w_in, w_out, pg2, pb2):
    B, S, D = x1.shape
    vec = _resident((1, D))
    return pl.pallas_call(
        _ffn2_kernel,
        out_shape=jax.ShapeDtypeStruct((B, S, D), f32),
        grid=(B, S // TM_FFN),
        in_specs=[_row_spec(TM_FFN, D), _row_spec(TM_FFN, attn.shape[-1]),
                  _row_spec(TM_FFN, yp.shape[-1]), _mod_spec(),
                  _resident(woa.shape), _resident(wop.shape), vec, vec,
                  _resident(w_in.shape), _resident(w_out.shape), vec, vec],
        out_specs=_row_spec(TM_FFN, D),
        compiler_params=pltpu.CompilerParams(
            dimension_semantics=("parallel", "parallel"), vmem_limit_bytes=VMEM_LIMIT),
        name="ffn2",
    )(x1, attn, yp, mod, woa, wop, pg1, pb1, w_in, w_out, pg2, pb2)


def _apply_rope(t, cos_t, sin_t):
    return t * cos_t + pltpu.roll(t, HEAD_PAD - QK_ROPE, 1) * sin_t


def _rope_slot_tables(pos_row, freq_col):
    t = pos_row.shape[1]
    ang = freq_col * pos_row
    c, s = jnp.cos(ang), jnp.sin(ang)
    zeros = jnp.zeros((QK_ROPE, t), f32)
    cos_rows = jnp.concatenate([jnp.ones((QK_NOPE, t), f32), c, c, zeros], axis=0)
    sin_rows = jnp.concatenate([jnp.zeros((QK_NOPE, t), f32), s, s, zeros], axis=0)
    return cos_rows, sin_rows


def _mix_kernel(x_ref, mod_ref, pos_ref, freq_ref, w_in_ref, qg_ref, kvg_ref, wuqt_ref, wuk_ref,
                wuvt_ref, wpool_ref, pbias_ref, pscale_ref,
                qt_ref, k_ref, vt_ref, yp_ref, tail_ref):
    st = pl.program_id(1)
    tm = x_ref.shape[1]
    shift, scale = mod_ref[0, 3:4, :], mod_ref[0, 4:5, :]
    ones_rows = (lax.broadcasted_iota(jnp.int32, (V_SLAB - V_DIM, TK), 0) == 0).astype(bf16)

    @pl.when(st == 0)
    def _():
        tail_ref[...] = jnp.zeros_like(tail_ref)

    tail = tail_ref[...]
    for t in range(tm // TK):
        rows = slice(t * TK, (t + 1) * TK)
        h = (x_ref[0, rows, :] * (1.0 + scale) + shift).astype(bf16)
        proj = jnp.dot(h, w_in_ref[...], preferred_element_type=f32)
        cq = proj[:, :Q_LORA]
        ckv = proj[:, Q_LORA:Q_LORA + KV_LORA]
        krb = proj[:, Q_LORA + KV_LORA:Q_LORA + KV_LORA + HEAD_PAD]
        u = proj[:, Q_LORA + KV_LORA + HEAD_PAD:]
        cos_rows, sin_rows = _rope_slot_tables(pos_ref[0, :, rows].astype(f32), freq_ref[...])
        cos_t, sin_t = cos_rows.T, sin_rows.T

        cqn = _rmsnorm(cq, qg_ref[...]).astype(bf16)
        q_t = lax.dot_general(wuqt_ref[...], cqn, _NT, preferred_element_type=f32)
        for hh in range(N_HEADS):
            sl = slice(hh * HEAD_PAD, (hh + 1) * HEAD_PAD)
            blk = q_t[sl, :]
            blk = blk * cos_rows + pltpu.roll(blk, HEAD_PAD - QK_ROPE, 0) * sin_rows
            qt_ref[0, t, sl, :] = (blk * (SM_SCALE * LOG2_E)).astype(bf16)
        ckvn = _rmsnorm(ckv, kvg_ref[...]).astype(bf16)
        k_all = jnp.dot(ckvn, wuk_ref[...], preferred_element_type=f32)
        v_t = lax.dot_general(wuvt_ref[...], ckvn, _NT, preferred_element_type=f32).astype(bf16)
        for hh in range(N_HEADS):
            vt_ref[0, t, hh * V_SLAB:hh * V_SLAB + V_DIM, :] = v_t[hh * V_DIM:(hh + 1) * V_DIM, :]
            vt_ref[0, t, hh * V_SLAB + V_DIM:(hh + 1) * V_SLAB, :] = ones_rows
        k_rope = _apply_rope(krb, cos_t, sin_t)
        for hh in range(N_HEADS):
            sl = slice(hh * HEAD_PAD, (hh + 1) * HEAD_PAD)
            k_ref[0, rows, sl] = (k_all[:, sl] + k_rope).astype(bf16)

        ext = jnp.concatenate([tail, u], axis=0)
        tail = u[TK - POOL_HALO:, :]
        t_idx = st * tm + t * TK + lax.broadcasted_iota(jnp.int32, (TK, 1), 0)
        groups = []
        for gi, w in enumerate(POOL_WINDOWS):
            e = ext[:, gi * POOL_GROUP:(gi + 1) * POOL_GROUP]
            sh = 1
            while sh < w:
                e = e + pltpu.roll(e, sh, 0)
                sh *= 2
            cnt = jnp.minimum(t_idx + 1, w).astype(f32)
            groups.append(e[POOL_HALO:, :] / cnt - u[:, gi * POOL_GROUP:(gi + 1) * POOL_GROUP])
        pooled = jnp.concatenate(groups, axis=1).astype(bf16)
        yp = jnp.dot(pooled, wpool_ref[...], preferred_element_type=f32) + pbias_ref[...]
        yp_ref[0, rows, :] = (yp * pscale_ref[...]).astype(bf16)
    tail_ref[...] = tail


def _mix(x1, mod, pos, freq, w_in, qg, kvg, wuq, wuk, wuvt, wpool, pbias, pscale):
    B, S, D = x1.shape
    assert TQ == TK
    tm = TM_MIX
    hw = N_HEADS * HEAD_PAD
    vw = N_HEADS * V_SLAB

    def const(shape):
        return pl.BlockSpec(shape, lambda b, i: (0,) * len(shape))

    return pl.pallas_call(
        _mix_kernel,
        out_shape=(jax.ShapeDtypeStruct((B, S // TQ, hw, TQ), bf16),
                   jax.ShapeDtypeStruct((B, S, hw), bf16),
                   jax.ShapeDtypeStruct((B, S // TK, vw, TK), bf16),
                   jax.ShapeDtypeStruct((B, S, POOL_WIDTH), bf16)),
        grid=(B, S // tm),
        in_specs=[_row_spec(tm, D), _mod_spec(),
                  pl.BlockSpec((1, 1, tm), lambda b, i: (b, 0, i)), const(freq.shape),
                  const(w_in.shape), const(qg.shape), const(kvg.shape), const(wuq.shape),
                  const(wuk.shape), const(wuvt.shape), const(wpool.shape), const(pbias.shape),
                  const(pscale.shape)],
        out_specs=(pl.BlockSpec((1, tm // TQ, hw, TQ), lambda b, i: (b, i, 0, 0)),
                   _row_spec(tm, hw),
                   pl.BlockSpec((1, tm // TK, vw, TK), lambda b, i: (b, i, 0, 0)),
                   _row_spec(tm, POOL_WIDTH)),
        scratch_shapes=[pltpu.VMEM((POOL_HALO, POOL_WIDTH), f32)],
        compiler_params=pltpu.CompilerParams(
            dimension_semantics=("parallel", "arbitrary"), vmem_limit_bytes=VMEM_LIMIT),
        name="mix_proj",
    )(x1, mod, pos, freq, w_in, qg, kvg, wuq, wuk, wuvt, wpool, pbias, pscale)


def _attn_kernel(qt_ref, k_ref, vt_ref, o_ref, m_ref, acc_ref, s_a, s_b, mt_a, mt_b):
    pair_idx = pl.program_id(1)
    i0, i1 = 2 * pair_idx, 2 * pair_idx + 1
    m_ref[...] = jnp.full_like(m_ref, -jnp.inf)
    acc_ref[...] = jnp.zeros_like(acc_ref)

    def qk_head(hh, sub, j, s_buf, mt_buf, masked):
        koff = pl.multiple_of(j * TK, TK)
        sl = slice(hh * HEAD_PAD, (hh + 1) * HEAD_PAD)
        s = jnp.dot(k_ref[0, pl.ds(koff, TK), sl], qt_ref[0, sub, sl, :],
                    preferred_element_type=f32)
        if masked:
            key_chunk = lax.broadcasted_iota(jnp.int32, (TK, TQ), 0) // CHUNK
            qry_chunk = lax.broadcasted_iota(jnp.int32, (TK, TQ), 1) // CHUNK
            s = jnp.where(key_chunk <= qry_chunk, s, -jnp.inf)
        s_buf[hh] = s
        mt_buf[hh] = jnp.max(s, axis=0, keepdims=True)

    def pv_head(hh, sub, j, s_buf, mt_buf):
        m_prev = m_ref[sub, hh]
        m_new = jnp.maximum(m_prev, mt_buf[hh])
        alpha = jnp.exp2(m_prev - m_new)
        p = jnp.exp2(s_buf[hh] - m_new).astype(bf16)
        v_t = vt_ref[0, j, hh * V_SLAB:(hh + 1) * V_SLAB, :]
        acc_ref[sub, hh] = alpha * acc_ref[sub, hh] + jnp.dot(v_t, p, preferred_element_type=f32)
        m_ref[sub, hh] = m_new

    def stage(qk=None, pv=None):
        for hh in range(N_HEADS):
            if qk is not None:
                qk_head(hh, *qk)
            if pv is not None:
                pv_head(hh, *pv)

    buf_a, buf_b = (s_a, mt_a), (s_b, mt_b)

    def key_tiles(sub, diag_tile, cur, nxt):
        def pair(u):
            stage(qk=(sub, 2 * u) + nxt + (False,),
                  pv=(sub, jnp.where(u == 0, diag_tile, 2 * u - 1)) + cur)
            stage(qk=(sub, 2 * u + 1) + cur + (False,), pv=(sub, 2 * u) + nxt)

        def body(v, carry):
            pair(2 * v)
            pair(2 * v + 1)
            return carry

        lax.fori_loop(0, pair_idx // 2, body, 0)

        @pl.when(pair_idx % 2 == 1)
        def _():
            pair(pair_idx - 1)

        return jnp.where(pair_idx == 0, diag_tile, i0 - 1)

    stage(qk=(0, i0) + buf_a + (True,))
    pending = key_tiles(0, i0, buf_a, buf_b)
    stage(qk=(1, i1) + buf_b + (True,), pv=(0, pending) + buf_a)
    pending = key_tiles(1, i1, buf_b, buf_a)
    stage(qk=(1, i0) + buf_a + (False,), pv=(1, pending) + buf_b)
    stage(pv=(1, i0) + buf_a)

    for sub in range(Q_PER_STEP):
        o_t = jnp.concatenate([acc_ref[sub, hh, :V_DIM, :] / acc_ref[sub, hh, V_DIM:V_DIM + 1, :]
                               for hh in range(N_HEADS)], axis=0)
        o_ref[0, sub * TQ:(sub + 1) * TQ, :] = o_t.T.astype(bf16)


def _attention(q_t, k, v_t):
    B, S, hw = k.shape
    vw = N_HEADS * V_DIM
    return pl.pallas_call(
        _attn_kernel,
        out_shape=jax.ShapeDtypeStruct((B, S, vw), bf16),
        grid=(B, S // (TQ * Q_PER_STEP)),
        in_specs=[pl.BlockSpec((1, Q_PER_STEP, hw, TQ), lambda b, i: (b, i, 0, 0)),
                  pl.BlockSpec((1, S, hw), lambda b, i: (b, 0, 0)),
                  pl.BlockSpec((1,) + v_t.shape[1:], lambda b, i: (b, 0, 0, 0))],
        out_specs=_row_spec(TQ * Q_PER_STEP, vw),
        scratch_shapes=[pltpu.VMEM((Q_PER_STEP, N_HEADS, 1, TQ), f32),
                        pltpu.VMEM((Q_PER_STEP, N_HEADS, V_SLAB, TQ), f32),
                        pltpu.VMEM((N_HEADS, TK, TQ), f32),
                        pltpu.VMEM((N_HEADS, TK, TQ), f32),
                        pltpu.VMEM((N_HEADS, 1, TQ), f32),
                        pltpu.VMEM((N_HEADS, 1, TQ), f32)],
        compiler_params=pltpu.CompilerParams(
            dimension_semantics=("parallel", "parallel"), vmem_limit_bytes=VMEM_LIMIT),
        name="attention",
    )(q_t, k, v_t)


def _rot_cols(w):
    half = w.shape[-1] // 2
    return jnp.concatenate([-w[..., half:], w[..., :half]], axis=-1)


def _head_slots(parts):
    k_dim = parts[0][0].shape[0]
    cols = []
    for blocks in parts:
        used = sum(b.shape[1] for b in blocks)
        cols.extend(blocks)
        if used < HEAD_PAD:
            cols.append(jnp.zeros((k_dim, HEAD_PAD - used), f32))
    return jnp.concatenate(cols, axis=1)


def kernel(x, c, positions, ln_in_g, ln_in_b, w_ada, b_ada, ffn1_w_in, ffn1_w_out, w_in, q_norm_g, kv_norm_g, w_uq, w_ukv, pool_w, pool_b, pool_scale, w_out, ffn2_w_in, ffn2_w_out, post_ln_g, post_ln_b):
    B, S, D = x.shape
    l = 0
    row = lambda a: a.reshape(1, -1)

    qh = QK_NOPE + QK_ROPE
    wuq = w_uq[l]
    wuq_pad = _head_slots([[wuq[:, h * qh:h * qh + QK_NOPE],
                            wuq[:, h * qh + QK_NOPE:(h + 1) * qh],
                            _rot_cols(wuq[:, h * qh + QK_NOPE:(h + 1) * qh])]
                           for h in range(N_HEADS)]).astype(bf16)
    kvh = QK_NOPE + V_DIM
    wukv = w_ukv[l]
    wuk_pad = _head_slots([[wukv[:, h * kvh:h * kvh + QK_NOPE]] for h in range(N_HEADS)]).astype(bf16)
    wuv_t = jnp.concatenate([wukv[:, h * kvh + QK_NOPE:(h + 1) * kvh] for h in range(N_HEADS)],
                            axis=1).T.astype(bf16)
    w_kr = w_in[l][:, Q_LORA + KV_LORA:Q_LORA + KV_LORA + QK_ROPE]
    w_in_pad = jnp.concatenate(
        [w_in[l][:, :Q_LORA + KV_LORA], jnp.zeros((D, QK_NOPE), f32), w_kr, _rot_cols(w_kr),
         w_in[l][:, Q_LORA + KV_LORA + QK_ROPE:]], axis=1).astype(bf16)
    wpool = jax.scipy.linalg.block_diag(*[pool_w[l, g] for g in range(len(POOL_WINDOWS))]).astype(bf16)
    wo = w_out[l]
    woa = wo[:N_HEADS * V_DIM].astype(bf16)
    wop = wo[N_HEADS * V_DIM:].astype(bf16)

    inv_freq = ROPE_THETA ** (-jnp.arange(0, QK_ROPE, 2, dtype=f32) / QK_ROPE)

    mod = _ada(c, w_ada[l], b_ada[l]).reshape(B, 9, D)

    x1 = _ffn1(x, mod, row(ln_in_g), row(ln_in_b), ffn1_w_in[l].astype(bf16),
               ffn1_w_out[l].astype(bf16), row(post_ln_g[l, 0]), row(post_ln_b[l, 0]))
    q_t, k, v_t, yp = _mix(x1, mod, positions.reshape(B, 1, S), inv_freq.reshape(-1, 1), w_in_pad,
                           row(q_norm_g[l]), row(kv_norm_g[l]),
                           wuq_pad.T, wuk_pad, wuv_t, wpool, row(pool_b[l]), row(pool_scale[l]))
    attn = _attention(q_t, k, v_t)
    return _ffn2(x1, attn, yp, mod, woa, wop, row(post_ln_g[l, 1]), row(post_ln_b[l, 1]),
                 ffn2_w_in[l].astype(bf16), ffn2_w_out[l].astype(bf16),
                 row(post_ln_g[l, 2]), row(post_ln_b[l, 2]))
```

```python
import jax
import jax.numpy as jnp
from jax import lax
from jax.experimental import pallas as pl
from jax.experimental.pallas import tpu as pltpu

f32 = jnp.float32
bf16 = jnp.bfloat16

D_MODEL = 1024
D_FF = 2816
N_HEADS = 8
QK_NOPE = 64
QK_ROPE = 32
V_DIM = 64
Q_LORA = 512
KV_LORA = 256
HEAD_PAD = 128
POOL_WINDOWS = (2, 4, 8, 16)
POOL_GROUP = 128
POOL_WIDTH = 512
POOL_HALO = 16
CHUNK = 64
ROPE_THETA = 10000.0
LN_EPS = 1e-5
RMS_EPS = 1e-6
ALPHA = 2.0 ** 0.25
SM_SCALE = (QK_NOPE + QK_ROPE) ** -0.5
LOG2_E = 1.4426950408889634
V_SLAB = 80
IN_PAD = Q_LORA + KV_LORA + HEAD_PAD + POOL_WIDTH

TM_FFN = 1024
TM_SUB = 256
TM_MIX = 1024
TQ = 256
TK = 256
Q_PER_STEP = 4
FF_CHUNKS = (1024, 1024, 768)
VMEM_LIMIT = 56 * 1024 * 1024

_NT = (((1,), (1,)), ((), ()))


def _layernorm(x, g, b):
    mu = jnp.mean(x, axis=-1, keepdims=True)
    xc = x - mu
    var = jnp.mean(xc * xc, axis=-1, keepdims=True)
    return xc * lax.rsqrt(var + LN_EPS) * g + b


def _rmsnorm(x, g):
    ms = jnp.mean(x * x, axis=-1, keepdims=True)
    return x * lax.rsqrt(ms + RMS_EPS) * g


def _swiglu(h, w_in_ref, w_out_ref):
    y = None
    off = 0
    for n in FF_CHUNKS:
        g = jnp.dot(h, w_in_ref[:, off:off + n], preferred_element_type=f32)
        u = jnp.dot(h, w_in_ref[:, D_FF + off:D_FF + off + n], preferred_element_type=f32)
        a = (g / (1.0 + jnp.exp(-g)) * u).astype(bf16)
        part = jnp.dot(a, w_out_ref[off:off + n, :], preferred_element_type=f32)
        y = part if y is None else y + part
        off += n
    return y


def _ada_kernel(c_ref, w_ref, b_ref, o_ref):
    c = c_ref[...]
    c_act = (c / (1.0 + jnp.exp(-c))).astype(bf16)
    o_ref[...] = jnp.dot(c_act, w_ref[...].astype(bf16), preferred_element_type=f32) + b_ref[...]


def _ada(c, w_ada, b_ada):
    B = c.shape[0]
    n_out = w_ada.shape[1]
    bn = D_MODEL
    return pl.pallas_call(
        _ada_kernel,
        out_shape=jax.ShapeDtypeStruct((B, n_out), f32),
        grid=(n_out // bn,),
        in_specs=[pl.BlockSpec((B, D_MODEL), lambda j: (0, 0)),
                  pl.BlockSpec((D_MODEL, bn), lambda j: (0, j)),
                  pl.BlockSpec((1, bn), lambda j: (0, j))],
        out_specs=pl.BlockSpec((B, bn), lambda j: (0, j)),
        name="ada",
    )(c, w_ada, b_ada.reshape(1, n_out))


def _ffn_block(x, mod_ref, first_row, w_in_ref, w_out_ref, pg_ref, pb_ref):
    shift, scale, gate = (mod_ref[0, first_row + r:first_row + r + 1, :] for r in range(3))
    h = (x * (1.0 + scale) + shift).astype(bf16)
    y = _swiglu(h, w_in_ref, w_out_ref)
    return _layernorm(ALPHA * x + 0.5 * gate * y, pg_ref[...], pb_ref[...])


def _ffn1_kernel(x_ref, mod_ref, lng_ref, lnb_ref, w_in_ref, w_out_ref, pg_ref, pb_ref, o_ref):
    for r0 in range(0, x_ref.shape[1], TM_SUB):
        rows = slice(r0, r0 + TM_SUB)
        x = _layernorm(x_ref[0, rows, :], lng_ref[...], lnb_ref[...])
        o_ref[0, rows, :] = _ffn_block(x, mod_ref, 0, w_in_ref, w_out_ref, pg_ref, pb_ref)


def _ffn2_kernel(x_ref, attn_ref, yp_ref, mod_ref, woa_ref, wop_ref, pg1_ref, pb1_ref,
                 w_in_ref, w_out_ref, pg2_ref, pb2_ref, o_ref):
    gate1 = mod_ref[0, 5:6, :]

    def mixer_residual(rows):
        y = (jnp.dot(attn_ref[0, rows, :], woa_ref[...], preferred_element_type=f32)
             + jnp.dot(yp_ref[0, rows, :], wop_ref[...], preferred_element_type=f32))
        return _layernorm(ALPHA * x_ref[0, rows, :] + gate1 * y, pg1_ref[...], pb1_ref[...])

    subs = [slice(r0, r0 + TM_SUB) for r0 in range(0, x_ref.shape[1], TM_SUB)]
    x_next = mixer_residual(subs[0])
    for n, rows in enumerate(subs):
        x = x_next
        if n + 1 < len(subs):
            x_next = mixer_residual(subs[n + 1])
        o_ref[0, rows, :] = _ffn_block(x, mod_ref, 6, w_in_ref, w_out_ref, pg2_ref, pb2_ref)


def _resident(shape):
    return pl.BlockSpec(shape, lambda b, i: (0,) * len(shape), pipeline_mode=pl.Buffered(1))


def _row_spec(tm, width):
    return pl.BlockSpec((1, tm, width), lambda b, i: (b, i, 0))


def _mod_spec():
    return pl.BlockSpec((1, 9, D_MODEL), lambda b, i: (b, 0, 0))


def _ffn1(x, mod, lng, lnb, w_in, w_out, pg, pb):
    B, S, D = x.shape
    vec = _resident((1, D))
    return pl.pallas_call(
        _ffn1_kernel,
        out_shape=jax.ShapeDtypeStruct((B, S, D), f32),
        grid=(B, S // TM_FFN),
        in_specs=[_row_spec(TM_FFN, D), _mod_spec(), vec, vec,
                  _resident(w_in.shape), _resident(w_out.shape), vec, vec],
        out_specs=_row_spec(TM_FFN, D),
        compiler_params=pltpu.CompilerParams(
            dimension_semantics=("parallel", "parallel"), vmem_limit_bytes=VMEM_LIMIT),
        name="ffn1",
    )(x, mod, lng, lnb, w_in, w_out, pg, pb)


def _ffn2(x1, attn, yp, mod, woa, wop, pg1, pb1, w_in, w_out, pg2, pb2):
    B, S, D = x1.shape
    vec = _resident((1, D))
    return pl.pallas_call(
        _ffn2_kernel,
        out_shape=jax.ShapeDtypeStruct((B, S, D), f32),
        grid=(B, S // TM_FFN),
        in_specs=[_row_spec(TM_FFN, D), _row_spec(TM_FFN, attn.shape[-1]),
                  _row_spec(TM_FFN, yp.shape[-1]), _mod_spec(),
                  _resident(woa.shape), _resident(wop.shape), vec, vec,
                  _resident(w_in.shape), _resident(w_out.shape), vec, vec],
        out_specs=_row_spec(TM_FFN, D),
        compiler_params=pltpu.CompilerParams(
            dimension_semantics=("parallel", "parallel"), vmem_limit_bytes=VMEM_LIMIT),
        name="ffn2",
    )(x1, attn, yp, mod, woa, wop, pg1, pb1, w_in, w_out, pg2, pb2)


def _apply_rope(t, cos_t, sin_t):
    return t * cos_t + pltpu.roll(t, HEAD_PAD - QK_ROPE, 1) * sin_t


def _rope_slot_tables(pos_row, freq_col):
    t = pos_row.shape[1]
    ang = freq_col * pos_row
    c, s = jnp.cos(ang), jnp.sin(ang)
    zeros = jnp.zeros((QK_ROPE, t), f32)
    cos_rows = jnp.concatenate([jnp.ones((QK_NOPE, t), f32), c, c, zeros], axis=0)
    sin_rows = jnp.concatenate([jnp.zeros((QK_NOPE, t), f32), s, s, zeros], axis=0)
    return cos_rows, sin_rows


def _mix_kernel(x_ref, mod_ref, pos_ref, freq_ref, w_in_ref, qg_ref, kvg_ref, wuqt_ref, wuk_ref,
                wuvt_ref, wpool_ref, pbias_ref, pscale_ref,
                qt_ref, k_ref, vt_ref, yp_ref, tail_ref):
    st = pl.program_id(1)
    tm = x_ref.shape[1]
    shift, scale = mod_ref[0, 3:4, :], mod_ref[0, 4:5, :]
    ones_rows = (lax.broadcasted_iota(jnp.int32, (V_SLAB - V_DIM, TK), 0) == 0).astype(bf16)

    @pl.when(st == 0)
    def _():
        tail_ref[...] = jnp.zeros_like(tail_ref)

    tail = tail_ref[...]
    for t in range(tm // TK):
        rows = slice(t * TK, (t + 1) * TK)
        h = (x_ref[0, rows, :] * (1.0 + scale) + shift).astype(bf16)
        proj = jnp.dot(h, w_in_ref[...], preferred_element_type=f32)
        cq = proj[:, :Q_LORA]
        ckv = proj[:, Q_LORA:Q_LORA + KV_LORA]
        krb = proj[:, Q_LORA + KV_LORA:Q_LORA + KV_LORA + HEAD_PAD]
        u = proj[:, Q_LORA + KV_LORA + HEAD_PAD:]
        cos_rows, sin_rows = _rope_slot_tables(pos_ref[0, :, rows].astype(f32), freq_ref[...])
        cos_t, sin_t = cos_rows.T, sin_rows.T

        cqn = _rmsnorm(cq, qg_ref[...]).astype(bf16)
        q_t = lax.dot_general(wuqt_ref[...], cqn, _NT, preferred_element_type=f32)
        for hh in range(N_HEADS):
            sl = slice(hh * HEAD_PAD, (hh + 1) * HEAD_PAD)
            blk = q_t[sl, :]
            blk = blk * cos_rows + pltpu.roll(blk, HEAD_PAD - QK_ROPE, 0) * sin_rows
            qt_ref[0, t, sl, :] = (blk * (SM_SCALE * LOG2_E)).astype(bf16)
        ckvn = _rmsnorm(ckv, kvg_ref[...]).astype(bf16)
        k_all = jnp.dot(ckvn, wuk_ref[...], preferred_element_type=f32)
        v_t = lax.dot_general(wuvt_ref[...], ckvn, _NT, preferred_element_type=f32).astype(bf16)
        for hh in range(N_HEADS):
            vt_ref[0, t, hh * V_SLAB:hh * V_SLAB + V_DIM, :] = v_t[hh * V_DIM:(hh + 1) * V_DIM, :]
            vt_ref[0, t, hh * V_SLAB + V_DIM:(hh + 1) * V_SLAB, :] = ones_rows
        k_rope = _apply_rope(krb, cos_t, sin_t)
        for hh in range(N_HEADS):
            sl = slice(hh * HEAD_PAD, (hh + 1) * HEAD_PAD)
            k_ref[0, rows, sl] = (k_all[:, sl] + k_rope).astype(bf16)

        ext = jnp.concatenate([tail, u], axis=0)
        tail = u[TK - POOL_HALO:, :]
        t_idx = st * tm + t * TK + lax.broadcasted_iota(jnp.int32, (TK, 1), 0)
        groups = []
        for gi, w in enumerate(POOL_WINDOWS):
            e = ext[:, gi * POOL_GROUP:(gi + 1) * POOL_GROUP]
            sh = 1
            while sh < w:
                e = e + pltpu.roll(e, sh, 0)
                sh *= 2
            cnt = jnp.minimum(t_idx + 1, w).astype(f32)
            groups.append(e[POOL_HALO:, :] / cnt - u[:, gi * POOL_GROUP:(gi + 1) * POOL_GROUP])
        pooled = jnp.concatenate(groups, axis=1).astype(bf16)
        yp = jnp.dot(pooled, wpool_ref[...], preferred_element_type=f32) + pbias_ref[...]
        yp_ref[0, rows, :] = (yp * pscale_ref[...]).astype(bf16)
    tail_ref[...] = tail


def _mix(x1, mod, pos, freq, w_in, qg, kvg, wuq, wuk, wuvt, wpool, pbias, pscale):
    B, S, D = x1.shape
    assert TQ == TK
    tm = TM_MIX
    hw = N_HEADS * HEAD_PAD
    vw = N_HEADS * V_SLAB

    def const(shape):
        return pl.BlockSpec(shape, lambda b, i: (0,) * len(shape))

    return pl.pallas_call(
        _mix_kernel,
        out_shape=(jax.ShapeDtypeStruct((B, S // TQ, hw, TQ), bf16),
                   jax.ShapeDtypeStruct((B, S, hw), bf16),
                   jax.ShapeDtypeStruct((B, S // TK, vw, TK), bf16),
                   jax.ShapeDtypeStruct((B, S, POOL_WIDTH), bf16)),
        grid=(B, S // tm),
        in_specs=[_row_spec(tm, D), _mod_spec(),
                  pl.BlockSpec((1, 1, tm), lambda b, i: (b, 0, i)), const(freq.shape),
                  const(w_in.shape), const(qg.shape), const(kvg.shape), const(wuq.shape),
                  const(wuk.shape), const(wuvt.shape), const(wpool.shape), const(pbias.shape),
                  const(pscale.shape)],
        out_specs=(pl.BlockSpec((1, tm // TQ, hw, TQ), lambda b, i: (b, i, 0, 0)),
                   _row_spec(tm, hw),
                   pl.BlockSpec((1, tm // TK, vw, TK), lambda b, i: (b, i, 0, 0)),
                   _row_spec(tm, POOL_WIDTH)),
        scratch_shapes=[pltpu.VMEM((POOL_HALO, POOL_WIDTH), f32)],
        compiler_params=pltpu.CompilerParams(
            dimension_semantics=("parallel", "arbitrary"), vmem_limit_bytes=VMEM_LIMIT),
        name="mix_proj",
    )(x1, mod, pos, freq, w_in, qg, kvg, wuq, wuk, wuvt, wpool, pbias, pscale)


def _attn_kernel(qt_ref, k_ref, vt_ref, o_ref, m_ref, acc_ref, s_a, s_b, mt_a, mt_b):
    base = pl.program_id(1) * Q_PER_STEP
    m_ref[...] = jnp.full_like(m_ref, -jnp.inf)
    acc_ref[...] = jnp.zeros_like(acc_ref)

    def qk_head(hh, sub, j, s_buf, mt_buf, masked):
        koff = pl.multiple_of(j * TK, TK)
        sl = slice(hh * HEAD_PAD, (hh + 1) * HEAD_PAD)
        s = jnp.dot(k_ref[0, pl.ds(koff, TK), sl], qt_ref[0, sub, sl, :],
                    preferred_element_type=f32)
        if masked:
            key_chunk = lax.broadcasted_iota(jnp.int32, (TK, TQ), 0) // CHUNK
            qry_chunk = lax.broadcasted_iota(jnp.int32, (TK, TQ), 1) // CHUNK
            s = jnp.where(key_chunk <= qry_chunk, s, -jnp.inf)
        s_buf[hh] = s
        mt_buf[hh] = jnp.max(s, axis=0, keepdims=True)

    def pv_head(hh, sub, j, s_buf, mt_buf):
        m_prev = m_ref[sub, hh]
        m_new = jnp.maximum(m_prev, mt_buf[hh])
        alpha = jnp.exp2(m_prev - m_new)
        p = jnp.exp2(s_buf[hh] - m_new).astype(bf16)
        v_t = vt_ref[0, j, hh * V_SLAB:(hh + 1) * V_SLAB, :]
        acc_ref[sub, hh] = alpha * acc_ref[sub, hh] + jnp.dot(v_t, p, preferred_element_type=f32)
        m_ref[sub, hh] = m_new

    def stage(qk=None, pv=None):
        for hh in range(N_HEADS):
            if qk is not None:
                qk_head(hh, *qk)
            if pv is not None:
                pv_head(hh, *pv)

    def key_tiles(sub, diag_tile, n_pairs, cur, nxt):
        def pair(u):
            stage(qk=(sub, 2 * u) + nxt + (False,),
                  pv=(sub, jnp.where(u == 0, diag_tile, 2 * u - 1)) + cur)
            stage(qk=(sub, 2 * u + 1) + cur + (False,), pv=(sub, 2 * u) + nxt)

        def body(v, carry):
            pair(2 * v)
            pair(2 * v + 1)
            return carry

        lax.fori_loop(0, n_pairs // 2, body, 0)

        @pl.when(n_pairs % 2 == 1)
        def _():
            pair(n_pairs - 1)

        return jnp.where(n_pairs == 0, diag_tile, 2 * n_pairs - 1)

    cur, other = (s_a, mt_a), (s_b, mt_b)
    stage(qk=(0, base) + cur + (True,))
    for sub in range(Q_PER_STEP):
        tile = base + sub
        pending = key_tiles(sub, tile, tile // 2, cur, other)
        if sub % 2 == 1:
            stage(qk=(sub, tile - 1) + other + (False,), pv=(sub, pending) + cur)
            last_tile, last_buf, free_buf = tile - 1, other, cur
        else:
            last_tile, last_buf, free_buf = pending, cur, other
        if sub + 1 < Q_PER_STEP:
            stage(qk=(sub + 1, tile + 1) + free_buf + (True,), pv=(sub, last_tile) + last_buf)
            cur, other = free_buf, last_buf
        else:
            stage(pv=(sub, last_tile) + last_buf)

    for sub in range(Q_PER_STEP):
        o_t = jnp.concatenate([acc_ref[sub, hh, :V_DIM, :] / acc_ref[sub, hh, V_DIM:V_DIM + 1, :]
                               for hh in range(N_HEADS)], axis=0)
        o_ref[0, sub * TQ:(sub + 1) * TQ, :] = o_t.T.astype(bf16)


def _attention(q_t, k, v_t):
    B, S, hw = k.shape
    vw = N_HEADS * V_DIM
    return pl.pallas_call(
        _attn_kernel,
        out_shape=jax.ShapeDtypeStruct((B, S, vw), bf16),
        grid=(B, S // (TQ * Q_PER_STEP)),
        in_specs=[pl.BlockSpec((1, Q_PER_STEP, hw, TQ), lambda b, i: (b, i, 0, 0)),
                  pl.BlockSpec((1, S, hw), lambda b, i: (b, 0, 0)),
                  pl.BlockSpec((1,) + v_t.shape[1:], lambda b, i: (b, 0, 0, 0))],
        out_specs=_row_spec(TQ * Q_PER_STEP, vw),
        scratch_shapes=[pltpu.VMEM((Q_PER_STEP, N_HEADS, 1, TQ), f32),
                        pltpu.VMEM((Q_PER_STEP, N_HEADS, V_SLAB, TQ), f32),
                        pltpu.VMEM((N_HEADS, TK, TQ), f32),
                        pltpu.VMEM((N_HEADS, TK, TQ), f32),
                        pltpu.VMEM((N_HEADS, 1, TQ), f32),
                        pltpu.VMEM((N_HEADS, 1, TQ), f32)],
        compiler_params=pltpu.CompilerParams(
            dimension_semantics=("parallel", "parallel"), vmem_limit_bytes=VMEM_LIMIT),
        name="attention",
    )(q_t, k, v_t)


def _rot_cols(w):
    half = w.shape[-1] // 2
    return jnp.concatenate([-w[..., half:], w[..., :half]], axis=-1)


def _head_slots(parts):
    k_dim = parts[0][0].shape[0]
    cols = []
    for blocks in parts:
        used = sum(b.shape[1] for b in blocks)
        cols.extend(blocks)
        if used < HEAD_PAD:
            cols.append(jnp.zeros((k_dim, HEAD_PAD - used), f32))
    return jnp.concatenate(cols, axis=1)


def kernel(x, c, positions, ln_in_g, ln_in_b, w_ada, b_ada, ffn1_w_in, ffn1_w_out, w_in, q_norm_g, kv_norm_g, w_uq, w_ukv, pool_w, pool_b, pool_scale, w_out, ffn2_w_in, ffn2_w_out, post_ln_g, post_ln_b):
    B, S, D = x.shape
    l = 0
    row = lambda a: a.reshape(1, -1)

    qh = QK_NOPE + QK_ROPE
    wuq = w_uq[l]
    wuq_pad = _head_slots([[wuq[:, h * qh:h * qh + QK_NOPE],
                            wuq[:, h * qh + QK_NOPE:(h + 1) * qh],
                            _rot_cols(wuq[:, h * qh + QK_NOPE:(h + 1) * qh])]
                           for h in range(N_HEADS)]).astype(bf16)
    kvh = QK_NOPE + V_DIM
    wukv = w_ukv[l]
    wuk_pad = _head_slots([[wukv[:, h * kvh:h * kvh + QK_NOPE]] for h in range(N_HEADS)]).astype(bf16)
    wuv_t = jnp.concatenate([wukv[:, h * kvh + QK_NOPE:(h + 1) * kvh] for h in range(N_HEADS)],
                            axis=1).T.astype(bf16)
    w_kr = w_in[l][:, Q_LORA + KV_LORA:Q_LORA + KV_LORA + QK_ROPE]
    w_in_pad = jnp.concatenate(
        [w_in[l][:, :Q_LORA + KV_LORA], jnp.zeros((D, QK_NOPE), f32), w_kr, _rot_cols(w_kr),
         w_in[l][:, Q_LORA + KV_LORA + QK_ROPE:]], axis=1).astype(bf16)
    wpool = jax.scipy.linalg.block_diag(*[pool_w[l, g] for g in range(len(POOL_WINDOWS))]).astype(bf16)
    wo = w_out[l]
    woa = wo[:N_HEADS * V_DIM].astype(bf16)
    wop = wo[N_HEADS * V_DIM:].astype(bf16)

    inv_freq = ROPE_THETA ** (-jnp.arange(0, QK_ROPE, 2, dtype=f32) / QK_ROPE)

    mod = _ada(c, w_ada[l], b_ada[l]).reshape(B, 9, D)

    x1 = _ffn1(x, mod, row(ln_in_g), row(ln_in_b), ffn1_w_in[l].astype(bf16),
               ffn1_w_out[l].astype(bf16), row(post_ln_g[l, 0]), row(post_ln_b[l, 0]))
    q_t, k, v_t, yp = _mix(x1, mod, positions.reshape(B, 1, S), inv_freq.reshape(-1, 1), w_in_pad,
                           row(q_norm_g[l]), row(kv_norm_g[l]),
                           wuq_pad.T, wuk_pad, wuv_t, wpool, row(pool_b[l]), row(pool_scale[l]))
    attn = _attention(q_t, k, v_t)
    return _ffn2(x1, attn, yp, mod, woa, wop, row(post_ln_g[l, 1]), row(post_ln_b[l, 1]),
                 ffn2_w_in[l].astype(bf16), ffn2_w_out[l].astype(bf16),
                 row(post_ln_g[l, 2]), row(post_ln_b[l, 2]))
```

```python
import jax
import jax.numpy as jnp
from jax import lax
from jax.experimental import pallas as pl
from jax.experimental.pallas import tpu as pltpu

f32 = jnp.float32
bf16 = jnp.bfloat16

D_MODEL = 1024
D_FF = 2816
N_HEADS = 8
QK_NOPE = 64
QK_ROPE = 32
V_DIM = 64
Q_LORA = 512
KV_LORA = 256
HEAD_PAD = 128
POOL_WINDOWS = (2, 4, 8, 16)
POOL_GROUP = 128
POOL_WIDTH = 512
POOL_HALO = 16
CHUNK = 64
ROPE_THETA = 10000.0
LN_EPS = 1e-5
RMS_EPS = 1e-6
ALPHA = 2.0 ** 0.25
SM_SCALE = (QK_NOPE + QK_ROPE) ** -0.5
LOG2_E = 1.4426950408889634
V_SLAB = 80
IN_PAD = Q_LORA + KV_LORA + HEAD_PAD + POOL_WIDTH

TM_FFN = 1024
TM_SUB = 256
TM_MIX = 1024
TQ = 256
TK = 256
Q_PER_STEP = 8
FF_CHUNKS = (1024, 1024, 768)
VMEM_LIMIT = 56 * 1024 * 1024

_NT = (((1,), (1,)), ((), ()))


def _layernorm(x, g, b):
    mu = jnp.mean(x, axis=-1, keepdims=True)
    xc = x - mu
    var = jnp.mean(xc * xc, axis=-1, keepdims=True)
    return xc * lax.rsqrt(var + LN_EPS) * g + b


def _rmsnorm(x, g):
    ms = jnp.mean(x * x, axis=-1, keepdims=True)
    return x * lax.rsqrt(ms + RMS_EPS) * g


def _swiglu(h, w_in_ref, w_out_ref):
    y = None
    off = 0
    for n in FF_CHUNKS:
        g = jnp.dot(h, w_in_ref[:, off:off + n], preferred_element_type=f32)
        u = jnp.dot(h, w_in_ref[:, D_FF + off:D_FF + off + n], preferred_element_type=f32)
        a = (g / (1.0 + jnp.exp(-g)) * u).astype(bf16)
        part = jnp.dot(a, w_out_ref[off:off + n, :], preferred_element_type=f32)
        y = part if y is None else y + part
        off += n
    return y


def _ada_kernel(c_ref, w_ref, b_ref, o_ref):
    c = c_ref[...]
    c_act = (c / (1.0 + jnp.exp(-c))).astype(bf16)
    o_ref[...] = jnp.dot(c_act, w_ref[...].astype(bf16), preferred_element_type=f32) + b_ref[...]


def _ada(c, w_ada, b_ada):
    B = c.shape[0]
    n_out = w_ada.shape[1]
    bn = D_MODEL
    return pl.pallas_call(
        _ada_kernel,
        out_shape=jax.ShapeDtypeStruct((B, n_out), f32),
        grid=(n_out // bn,),
        in_specs=[pl.BlockSpec((B, D_MODEL), lambda j: (0, 0)),
                  pl.BlockSpec((D_MODEL, bn), lambda j: (0, j)),
                  pl.BlockSpec((1, bn), lambda j: (0, j))],
        out_specs=pl.BlockSpec((B, bn), lambda j: (0, j)),
        name="ada",
    )(c, w_ada, b_ada.reshape(1, n_out))


def _ffn_block(x, mod_ref, first_row, w_in_ref, w_out_ref, pg_ref, pb_ref):
    shift, scale, gate = (mod_ref[0, first_row + r:first_row + r + 1, :] for r in range(3))
    h = (x * (1.0 + scale) + shift).astype(bf16)
    y = _swiglu(h, w_in_ref, w_out_ref)
    return _layernorm(ALPHA * x + 0.5 * gate * y, pg_ref[...], pb_ref[...])


def _ffn1_kernel(x_ref, mod_ref, lng_ref, lnb_ref, w_in_ref, w_out_ref, pg_ref, pb_ref, o_ref):
    for r0 in range(0, x_ref.shape[1], TM_SUB):
        rows = slice(r0, r0 + TM_SUB)
        x = _layernorm(x_ref[0, rows, :], lng_ref[...], lnb_ref[...])
        o_ref[0, rows, :] = _ffn_block(x, mod_ref, 0, w_in_ref, w_out_ref, pg_ref, pb_ref)


def _ffn2_kernel(x_ref, attn_ref, yp_ref, mod_ref, woa_ref, wop_ref, pg1_ref, pb1_ref,
                 w_in_ref, w_out_ref, pg2_ref, pb2_ref, o_ref):
    gate1 = mod_ref[0, 5:6, :]

    def mixer_residual(rows):
        y = (jnp.dot(attn_ref[0, rows, :], woa_ref[...], preferred_element_type=f32)
             + jnp.dot(yp_ref[0, rows, :], wop_ref[...], preferred_element_type=f32))
        return _layernorm(ALPHA * x_ref[0, rows, :] + gate1 * y, pg1_ref[...], pb1_ref[...])

    subs = [slice(r0, r0 + TM_SUB) for r0 in range(0, x_ref.shape[1], TM_SUB)]
    x_next = mixer_residual(subs[0])
    for n, rows in enumerate(subs):
        x = x_next
        if n + 1 < len(subs):
            x_next = mixer_residual(subs[n + 1])
        o_ref[0, rows, :] = _ffn_block(x, mod_ref, 6, w_in_ref, w_out_ref, pg2_ref, pb2_ref)


def _resident(shape):
    return pl.BlockSpec(shape, lambda b, i: (0,) * len(shape), pipeline_mode=pl.Buffered(1))


def _row_spec(tm, width):
    return pl.BlockSpec((1, tm, width), lambda b, i: (b, i, 0))


def _mod_spec():
    return pl.BlockSpec((1, 9, D_MODEL), lambda b, i: (b, 0, 0))


def _ffn1(x, mod, lng, lnb, w_in, w_out, pg, pb):
    B, S, D = x.shape
    vec = _resident((1, D))
    return pl.pallas_call(
        _ffn1_kernel,
        out_shape=jax.ShapeDtypeStruct((B, S, D), f32),
        grid=(B, S // TM_FFN),
        in_specs=[_row_spec(TM_FFN, D), _mod_spec(), vec, vec,
                  _resident(w_in.shape), _resident(w_out.shape), vec, vec],
        out_specs=_row_spec(TM_FFN, D),
        compiler_params=pltpu.CompilerParams(
            dimension_semantics=("parallel", "parallel"), vmem_limit_bytes=VMEM_LIMIT),
        name="ffn1",
    )(x, mod, lng, lnb, w_in, w_out, pg, pb)


def _ffn2(x1, attn, yp, mod, woa, wop, pg1, pb1, w_in, w_out, pg2, pb2):
    B, S, D = x1.shape
    vec = _resident((1, D))
    return pl.pallas_call(
        _ffn2_kernel,
        out_shape=jax.ShapeDtypeStruct((B, S, D), f32),
        grid=(B, S // TM_FFN),
        in_specs=[_row_spec(TM_FFN, D), _row_spec(TM_FFN, attn.shape[-1]),
                  _row_spec(TM_FFN, yp.shape[-1]), _mod_spec(),
                  _resident(woa.shape), _resident(wop.shape), vec, vec,
                  _resident(w_in.shape), _resident(w_out.shape), vec, vec],
        out_specs=_row_spec(TM_FFN, D),
        compiler_params=pltpu.CompilerParams(
            dimension_semantics=("parallel", "parallel"), vmem_limit_bytes=VMEM_LIMIT),
        name="ffn2",
    )(x1, attn, yp, mod, woa, wop, pg1, pb1, w_in, w_out, pg2, pb2)


def _apply_rope(t, cos_t, sin_t):
    return t * cos_t + pltpu.roll(t, HEAD_PAD - QK_ROPE, 1) * sin_t


def _rope_slot_tables(pos_row, freq_col):
    t = pos_row.shape[1]
    ang = freq_col * pos_row
    c, s = jnp.cos(ang), jnp.sin(ang)
    zeros = jnp.zeros((QK_ROPE, t), f32)
    cos_rows = jnp.concatenate([jnp.ones((QK_NOPE, t), f32), c, c, zeros], axis=0)
    sin_rows = jnp.concatenate([jnp.zeros((QK_NOPE, t), f32), s, s, zeros], axis=0)
    return cos_rows, sin_rows


def _mix_kernel(x_ref, mod_ref, pos_ref, freq_ref, w_in_ref, qg_ref, kvg_ref, wuqt_ref, wuk_ref,
                wuvt_ref, wpool_ref, pbias_ref, pscale_ref,
                qt_ref, k_ref, vt_ref, yp_ref, tail_ref):
    st = pl.program_id(1)
    tm = x_ref.shape[1]
    shift, scale = mod_ref[0, 3:4, :], mod_ref[0, 4:5, :]
    ones_rows = (lax.broadcasted_iota(jnp.int32, (V_SLAB - V_DIM, TK), 0) == 0).astype(bf16)

    @pl.when(st == 0)
    def _():
        tail_ref[...] = jnp.zeros_like(tail_ref)

    tail = tail_ref[...]
    for t in range(tm // TK):
        rows = slice(t * TK, (t + 1) * TK)
        h = (x_ref[0, rows, :] * (1.0 + scale) + shift).astype(bf16)
        proj = jnp.dot(h, w_in_ref[...], preferred_element_type=f32)
        cq = proj[:, :Q_LORA]
        ckv = proj[:, Q_LORA:Q_LORA + KV_LORA]
        krb = proj[:, Q_LORA + KV_LORA:Q_LORA + KV_LORA + HEAD_PAD]
        u = proj[:, Q_LORA + KV_LORA + HEAD_PAD:]
        cos_rows, sin_rows = _rope_slot_tables(pos_ref[0, :, rows].astype(f32), freq_ref[...])
        cos_t, sin_t = cos_rows.T, sin_rows.T

        cqn = _rmsnorm(cq, qg_ref[...]).astype(bf16)
        q_t = lax.dot_general(wuqt_ref[...], cqn, _NT, preferred_element_type=f32)
        for hh in range(N_HEADS):
            sl = slice(hh * HEAD_PAD, (hh + 1) * HEAD_PAD)
            blk = q_t[sl, :]
            blk = blk * cos_rows + pltpu.roll(blk, HEAD_PAD - QK_ROPE, 0) * sin_rows
            qt_ref[0, t, sl, :] = (blk * (SM_SCALE * LOG2_E)).astype(bf16)
        ckvn = _rmsnorm(ckv, kvg_ref[...]).astype(bf16)
        k_all = jnp.dot(ckvn, wuk_ref[...], preferred_element_type=f32)
        v_t = lax.dot_general(wuvt_ref[...], ckvn, _NT, preferred_element_type=f32).astype(bf16)
        for hh in range(N_HEADS):
            vt_ref[0, t, hh * V_SLAB:hh * V_SLAB + V_DIM, :] = v_t[hh * V_DIM:(hh + 1) * V_DIM, :]
            vt_ref[0, t, hh * V_SLAB + V_DIM:(hh + 1) * V_SLAB, :] = ones_rows
        k_rope = _apply_rope(krb, cos_t, sin_t)
        for hh in range(N_HEADS):
            sl = slice(hh * HEAD_PAD, (hh + 1) * HEAD_PAD)
            k_ref[0, rows, sl] = (k_all[:, sl] + k_rope).astype(bf16)

        ext = jnp.concatenate([tail, u], axis=0)
        tail = u[TK - POOL_HALO:, :]
        t_idx = st * tm + t * TK + lax.broadcasted_iota(jnp.int32, (TK, 1), 0)
        groups = []
        for gi, w in enumerate(POOL_WINDOWS):
            e = ext[:, gi * POOL_GROUP:(gi + 1) * POOL_GROUP]
            sh = 1
            while sh < w:
                e = e + pltpu.roll(e, sh, 0)
                sh *= 2
            cnt = jnp.minimum(t_idx + 1, w).astype(f32)
            groups.append(e[POOL_HALO:, :] / cnt - u[:, gi * POOL_GROUP:(gi + 1) * POOL_GROUP])
        pooled = jnp.concatenate(groups, axis=1).astype(bf16)
        yp = jnp.dot(pooled, wpool_ref[...], preferred_element_type=f32) + pbias_ref[...]
        yp_ref[0, rows, :] = (yp * pscale_ref[...]).astype(bf16)
    tail_ref[...] = tail


def _mix(x1, mod, pos, freq, w_in, qg, kvg, wuq, wuk, wuvt, wpool, pbias, pscale):
    B, S, D = x1.shape
    assert TQ == TK
    tm = TM_MIX
    hw = N_HEADS * HEAD_PAD
    vw = N_HEADS * V_SLAB

    def const(shape):
        return pl.BlockSpec(shape, lambda b, i: (0,) * len(shape))

    return pl.pallas_call(
        _mix_kernel,
        out_shape=(jax.ShapeDtypeStruct((B, S // TQ, hw, TQ), bf16),
                   jax.ShapeDtypeStruct((B, S, hw), bf16),
                   jax.ShapeDtypeStruct((B, S // TK, vw, TK), bf16),
                   jax.ShapeDtypeStruct((B, S, POOL_WIDTH), bf16)),
        grid=(B, S // tm),
        in_specs=[_row_spec(tm, D), _mod_spec(),
                  pl.BlockSpec((1, 1, tm), lambda b, i: (b, 0, i)), const(freq.shape),
                  const(w_in.shape), const(qg.shape), const(kvg.shape), const(wuq.shape),
                  const(wuk.shape), const(wuvt.shape), const(wpool.shape), const(pbias.shape),
                  const(pscale.shape)],
        out_specs=(pl.BlockSpec((1, tm // TQ, hw, TQ), lambda b, i: (b, i, 0, 0)),
                   _row_spec(tm, hw),
                   pl.BlockSpec((1, tm // TK, vw, TK), lambda b, i: (b, i, 0, 0)),
                   _row_spec(tm, POOL_WIDTH)),
        scratch_shapes=[pltpu.VMEM((POOL_HALO, POOL_WIDTH), f32)],
        compiler_params=pltpu.CompilerParams(
            dimension_semantics=("parallel", "arbitrary"), vmem_limit_bytes=VMEM_LIMIT),
        name="mix_proj",
    )(x1, mod, pos, freq, w_in, qg, kvg, wuq, wuk, wuvt, wpool, pbias, pscale)


def _attn_kernel(qt_ref, k_ref, vt_ref, o_ref, m_ref, acc_ref, s_a, s_b, mt_a, mt_b):
    base = pl.program_id(1) * Q_PER_STEP
    m_ref[...] = jnp.full_like(m_ref, -jnp.inf)
    acc_ref[...] = jnp.zeros_like(acc_ref)

    def qk_head(hh, sub, j, s_buf, mt_buf, masked):
        koff = pl.multiple_of(j * TK, TK)
        sl = slice(hh * HEAD_PAD, (hh + 1) * HEAD_PAD)
        s = jnp.dot(k_ref[0, pl.ds(koff, TK), sl], qt_ref[0, sub, sl, :],
                    preferred_element_type=f32)
        if masked:
            key_chunk = lax.broadcasted_iota(jnp.int32, (TK, TQ), 0) // CHUNK
            qry_chunk = lax.broadcasted_iota(jnp.int32, (TK, TQ), 1) // CHUNK
            s = jnp.where(key_chunk <= qry_chunk, s, -jnp.inf)
        s_buf[hh] = s
        mt_buf[hh] = jnp.max(s, axis=0, keepdims=True)

    def pv_head(hh, sub, j, s_buf, mt_buf):
        m_prev = m_ref[sub, hh]
        m_new = jnp.maximum(m_prev, mt_buf[hh])
        alpha = jnp.exp2(m_prev - m_new)
        p = jnp.exp2(s_buf[hh] - m_new).astype(bf16)
        v_t = vt_ref[0, j, hh * V_SLAB:(hh + 1) * V_SLAB, :]
        acc_ref[sub, hh] = alpha * acc_ref[sub, hh] + jnp.dot(v_t, p, preferred_element_type=f32)
        m_ref[sub, hh] = m_new

    def stage(qk=None, pv=None):
        for hh in range(N_HEADS):
            if qk is not None:
                qk_head(hh, *qk)
            if pv is not None:
                pv_head(hh, *pv)

    def key_tiles(sub, diag_tile, n_pairs, cur, nxt):
        def pair(u):
            stage(qk=(sub, 2 * u) + nxt + (False,),
                  pv=(sub, jnp.where(u == 0, diag_tile, 2 * u - 1)) + cur)
            stage(qk=(sub, 2 * u + 1) + cur + (False,), pv=(sub, 2 * u) + nxt)

        def body(v, carry):
            pair(2 * v)
            pair(2 * v + 1)
            return carry

        lax.fori_loop(0, n_pairs // 2, body, 0)

        @pl.when(n_pairs % 2 == 1)
        def _():
            pair(n_pairs - 1)

        return jnp.where(n_pairs == 0, diag_tile, 2 * n_pairs - 1)

    cur, other = (s_a, mt_a), (s_b, mt_b)
    stage(qk=(0, base) + cur + (True,))
    for sub in range(Q_PER_STEP):
        tile = base + sub
        pending = key_tiles(sub, tile, tile // 2, cur, other)
        if sub % 2 == 1:
            stage(qk=(sub, tile - 1) + other + (False,), pv=(sub, pending) + cur)
            last_tile, last_buf, free_buf = tile - 1, other, cur
        else:
            last_tile, last_buf, free_buf = pending, cur, other
        if sub + 1 < Q_PER_STEP:
            stage(qk=(sub + 1, tile + 1) + free_buf + (True,), pv=(sub, last_tile) + last_buf)
            cur, other = free_buf, last_buf
        else:
            stage(pv=(sub, last_tile) + last_buf)

    for sub in range(Q_PER_STEP):
        o_t = jnp.concatenate([acc_ref[sub, hh, :V_DIM, :] / acc_ref[sub, hh, V_DIM:V_DIM + 1, :]
                               for hh in range(N_HEADS)], axis=0)
        o_ref[0, sub * TQ:(sub + 1) * TQ, :] = o_t.T.astype(bf16)


def _attention(q_t, k, v_t):
    B, S, hw = k.shape
    vw = N_HEADS * V_DIM
    return pl.pallas_call(
        _attn_kernel,
        out_shape=jax.ShapeDtypeStruct((B, S, vw), bf16),
        grid=(B, S // (TQ * Q_PER_STEP)),
        in_specs=[pl.BlockSpec((1, Q_PER_STEP, hw, TQ), lambda b, i: (b, i, 0, 0)),
                  pl.BlockSpec((1, S, hw), lambda b, i: (b, 0, 0)),
                  pl.BlockSpec((1,) + v_t.shape[1:], lambda b, i: (b, 0, 0, 0))],
        out_specs=_row_spec(TQ * Q_PER_STEP, vw),
        scratch_shapes=[pltpu.VMEM((Q_PER_STEP, N_HEADS, 1, TQ), f32),
                        pltpu.VMEM((Q_PER_STEP, N_HEADS, V_SLAB, TQ), f32),
                        pltpu.VMEM((N_HEADS, TK, TQ), f32),
                        pltpu.VMEM((N_HEADS, TK, TQ), f32),
                        pltpu.VMEM((N_HEADS, 1, TQ), f32),
                        pltpu.VMEM((N_HEADS, 1, TQ), f32)],
        compiler_params=pltpu.CompilerParams(
            dimension_semantics=("parallel", "parallel"), vmem_limit_bytes=VMEM_LIMIT),
        name="attention",
    )(q_t, k, v_t)


def _rot_cols(w):
    half = w.shape[-1] // 2
    return jnp.concatenate([-w[..., half:], w[..., :half]], axis=-1)


def _head_slots(parts):
    k_dim = parts[0][0].shape[0]
    cols = []
    for blocks in parts:
        used = sum(b.shape[1] for b in blocks)
        cols.extend(blocks)
        if used < HEAD_PAD:
            cols.append(jnp.zeros((k_dim, HEAD_PAD - used), f32))
    return jnp.concatenate(cols, axis=1)


def kernel(x, c, positions, ln_in_g, ln_in_b, w_ada, b_ada, ffn1_w_in, ffn1_w_out, w_in, q_norm_g, kv_norm_g, w_uq, w_ukv, pool_w, pool_b, pool_scale, w_out, ffn2_w_in, ffn2_w_out, post_ln_g, post_ln_b):
    B, S, D = x.shape
    l = 0
    row = lambda a: a.reshape(1, -1)

    qh = QK_NOPE + QK_ROPE
    wuq = w_uq[l]
    wuq_pad = _head_slots([[wuq[:, h * qh:h * qh + QK_NOPE],
                            wuq[:, h * qh + QK_NOPE:(h + 1) * qh],
                            _rot_cols(wuq[:, h * qh + QK_NOPE:(h + 1) * qh])]
                           for h in range(N_HEADS)]).astype(bf16)
    kvh = QK_NOPE + V_DIM
    wukv = w_ukv[l]
    wuk_pad = _head_slots([[wukv[:, h * kvh:h * kvh + QK_NOPE]] for h in range(N_HEADS)]).astype(bf16)
    wuv_t = jnp.concatenate([wukv[:, h * kvh + QK_NOPE:(h + 1) * kvh] for h in range(N_HEADS)],
                            axis=1).T.astype(bf16)
    w_kr = w_in[l][:, Q_LORA + KV_LORA:Q_LORA + KV_LORA + QK_ROPE]
    w_in_pad = jnp.concatenate(
        [w_in[l][:, :Q_LORA + KV_LORA], jnp.zeros((D, QK_NOPE), f32), w_kr, _rot_cols(w_kr),
         w_in[l][:, Q_LORA + KV_LORA + QK_ROPE:]], axis=1).astype(bf16)
    wpool = jax.scipy.linalg.block_diag(*[pool_w[l, g] for g in range(len(POOL_WINDOWS))]).astype(bf16)
    wo = w_out[l]
    woa = wo[:N_HEADS * V_DIM].astype(bf16)
    wop = wo[N_HEADS * V_DIM:].astype(bf16)

    inv_freq = ROPE_THETA ** (-jnp.arange(0, QK_ROPE, 2, dtype=f32) / QK_ROPE)

    mod = _ada(c, w_ada[l], b_ada[l]).reshape(B, 9, D)

    x1 = _ffn1(x, mod, row(ln_in_g), row(ln_in_b), ffn1_w_in[l].astype(bf16),
               ffn1_w_out[l].astype(bf16), row(post_ln_g[l, 0]), row(post_ln_b[l, 0]))
    q_t, k, v_t, yp = _mix(x1, mod, positions.reshape(B, 1, S), inv_freq.reshape(-1, 1), w_in_pad,
                           row(q_norm_g[l]), row(kv_norm_g[l]),
                           wuq_pad.T, wuk_pad, wuv_t, wpool, row(pool_b[l]), row(pool_scale[l]))
    attn = _attention(q_t, k, v_t)
    return _ffn2(x1, attn, yp, mod, woa, wop, row(post_ln_g[l, 1]), row(post_ln_b[l, 1]),
                 ffn2_w_in[l].astype(bf16), ffn2_w_out[l].astype(bf16),
                 row(post_ln_g[l, 2]), row(post_ln_b[l, 2]))
```

```python
import jax
import jax.numpy as jnp
from jax import lax
from jax.experimental import pallas as pl
from jax.experimental.pallas import tpu as pltpu

f32 = jnp.float32
bf16 = jnp.bfloat16

D_MODEL = 1024
D_FF = 2816
N_HEADS = 8
QK_NOPE = 64
QK_ROPE = 32
V_DIM = 64
Q_LORA = 512
KV_LORA = 256
HEAD_PAD = 128
POOL_WINDOWS = (2, 4, 8, 16)
POOL_GROUP = 128
POOL_WIDTH = 512
POOL_HALO = 16
CHUNK = 64
ROPE_THETA = 10000.0
LN_EPS = 1e-5
RMS_EPS = 1e-6
ALPHA = 2.0 ** 0.25
SM_SCALE = (QK_NOPE + QK_ROPE) ** -0.5
LOG2_E = 1.4426950408889634
V_SLAB = 80
IN_PAD = Q_LORA + KV_LORA + HEAD_PAD + POOL_WIDTH

TM_FFN = 1024
TM_SUB = 256
TM_MIX = 1024
TQ = 256
TK = 256
Q_PER_STEP = 4
FF_CHUNKS = (1024, 1024, 768)
VMEM_LIMIT = 56 * 1024 * 1024

_NT = (((1,), (1,)), ((), ()))


def _layernorm(x, g, b):
    mu = jnp.mean(x, axis=-1, keepdims=True)
    xc = x - mu
    var = jnp.mean(xc * xc, axis=-1, keepdims=True)
    return xc * lax.rsqrt(var + LN_EPS) * g + b


def _rmsnorm(x, g):
    ms = jnp.mean(x * x, axis=-1, keepdims=True)
    return x * lax.rsqrt(ms + RMS_EPS) * g


def _swiglu(h, w_in_ref, w_out_ref):
    y = None
    off = 0
    for n in FF_CHUNKS:
        g = jnp.dot(h, w_in_ref[:, off:off + n], preferred_element_type=f32)
        u = jnp.dot(h, w_in_ref[:, D_FF + off:D_FF + off + n], preferred_element_type=f32)
        a = (g / (1.0 + jnp.exp(-g)) * u).astype(bf16)
        part = jnp.dot(a, w_out_ref[off:off + n, :], preferred_element_type=f32)
        y = part if y is None else y + part
        off += n
    return y


def _ada_kernel(c_ref, w_ref, b_ref, o_ref):
    c = c_ref[...]
    c_act = (c / (1.0 + jnp.exp(-c))).astype(bf16)
    o_ref[...] = jnp.dot(c_act, w_ref[...].astype(bf16), preferred_element_type=f32) + b_ref[...]


def _ada(c, w_ada, b_ada):
    B = c.shape[0]
    n_out = w_ada.shape[1]
    bn = D_MODEL
    return pl.pallas_call(
        _ada_kernel,
        out_shape=jax.ShapeDtypeStruct((B, n_out), f32),
        grid=(n_out // bn,),
        in_specs=[pl.BlockSpec((B, D_MODEL), lambda j: (0, 0)),
                  pl.BlockSpec((D_MODEL, bn), lambda j: (0, j)),
                  pl.BlockSpec((1, bn), lambda j: (0, j))],
        out_specs=pl.BlockSpec((B, bn), lambda j: (0, j)),
        name="ada",
    )(c, w_ada, b_ada.reshape(1, n_out))


def _ffn_block(x, mod_ref, first_row, w_in_ref, w_out_ref, pg_ref, pb_ref):
    shift, scale, gate = (mod_ref[0, first_row + r:first_row + r + 1, :] for r in range(3))
    h = (x * (1.0 + scale) + shift).astype(bf16)
    y = _swiglu(h, w_in_ref, w_out_ref)
    return _layernorm(ALPHA * x + 0.5 * gate * y, pg_ref[...], pb_ref[...])


def _ffn1_kernel(x_ref, mod_ref, lng_ref, lnb_ref, w_in_ref, w_out_ref, pg_ref, pb_ref, o_ref):
    for r0 in range(0, x_ref.shape[1], TM_SUB):
        rows = slice(r0, r0 + TM_SUB)
        x = _layernorm(x_ref[0, rows, :], lng_ref[...], lnb_ref[...])
        o_ref[0, rows, :] = _ffn_block(x, mod_ref, 0, w_in_ref, w_out_ref, pg_ref, pb_ref)


def _ffn2_kernel(x_ref, attn_ref, yp_ref, mod_ref, woa_ref, wop_ref, pg1_ref, pb1_ref,
                 w_in_ref, w_out_ref, pg2_ref, pb2_ref, o_ref):
    gate1 = mod_ref[0, 5:6, :]

    def mixer_residual(rows):
        y = (jnp.dot(attn_ref[0, rows, :], woa_ref[...], preferred_element_type=f32)
             + jnp.dot(yp_ref[0, rows, :], wop_ref[...], preferred_element_type=f32))
        return _layernorm(ALPHA * x_ref[0, rows, :] + gate1 * y, pg1_ref[...], pb1_ref[...])

    subs = [slice(r0, r0 + TM_SUB) for r0 in range(0, x_ref.shape[1], TM_SUB)]
    x_next = mixer_residual(subs[0])
    for n, rows in enumerate(subs):
        x = x_next
        if n + 1 < len(subs):
            x_next = mixer_residual(subs[n + 1])
        o_ref[0, rows, :] = _ffn_block(x, mod_ref, 6, w_in_ref, w_out_ref, pg2_ref, pb2_ref)


def _resident(shape):
    return pl.BlockSpec(shape, lambda b, i: (0,) * len(shape), pipeline_mode=pl.Buffered(1))


def _row_spec(tm, width):
    return pl.BlockSpec((1, tm, width), lambda b, i: (b, i, 0))


def _mod_spec():
    return pl.BlockSpec((1, 9, D_MODEL), lambda b, i: (b, 0, 0))


def _ffn1(x, mod, lng, lnb, w_in, w_out, pg, pb):
    B, S, D = x.shape
    vec = _resident((1, D))
    return pl.pallas_call(
        _ffn1_kernel,
        out_shape=jax.ShapeDtypeStruct((B, S, D), f32),
        grid=(B, S // TM_FFN),
        in_specs=[_row_spec(TM_FFN, D), _mod_spec(), vec, vec,
                  _resident(w_in.shape), _resident(w_out.shape), vec, vec],
        out_specs=_row_spec(TM_FFN, D),
        compiler_params=pltpu.CompilerParams(
            dimension_semantics=("parallel", "parallel"), vmem_limit_bytes=VMEM_LIMIT),
        name="ffn1",
    )(x, mod, lng, lnb, w_in, w_out, pg, pb)


def _ffn2(x1, attn, yp, mod, woa, wop, pg1, pb1, w_in, w_out, pg2, pb2):
    B, S, D = x1.shape
    vec = _resident((1, D))
    return pl.pallas_call(
        _ffn2_kernel,
        out_shape=jax.ShapeDtypeStruct((B, S, D), f32),
        grid=(B, S // TM_FFN),
        in_specs=[_row_spec(TM_FFN, D), _row_spec(TM_FFN, attn.shape[-1]),
                  _row_spec(TM_FFN, yp.shape[-1]), _mod_spec(),
                  _resident(woa.shape), _resident(wop.shape), vec, vec,
                  _resident(w_in.shape), _resident(w_out.shape), vec, vec],
        out_specs=_row_spec(TM_FFN, D),
        compiler_params=pltpu.CompilerParams(
            dimension_semantics=("parallel", "parallel"), vmem_limit_bytes=VMEM_LIMIT),
        name="ffn2",
    )(x1, attn, yp, mod, woa, wop, pg1, pb1, w_in, w_out, pg2, pb2)


def _apply_rope(t, cos_t, sin_t):
    return t * cos_t + pltpu.roll(t, HEAD_PAD - QK_ROPE, 1) * sin_t


def _rope_slot_tables(pos_row, freq_col):
    t = pos_row.shape[1]
    ang = freq_col * pos_row
    c, s = jnp.cos(ang), jnp.sin(ang)
    zeros = jnp.zeros((QK_ROPE, t), f32)
    cos_rows = jnp.concatenate([jnp.ones((QK_NOPE, t), f32), c, c, zeros], axis=0)
    sin_rows = jnp.concatenate([jnp.zeros((QK_NOPE, t), f32), s, s, zeros], axis=0)
    return cos_rows, sin_rows


def _mix_kernel(x_ref, mod_ref, pos_ref, freq_ref, w_in_ref, qg_ref, kvg_ref, wuqt_ref, wuk_ref,
                wuvt_ref, wpool_ref, pbias_ref, pscale_ref,
                qt_ref, k_ref, vt_ref, yp_ref, tail_ref):
    st = pl.program_id(1)
    tm = x_ref.shape[1]
    shift, scale = mod_ref[0, 3:4, :], mod_ref[0, 4:5, :]
    ones_rows = (lax.broadcasted_iota(jnp.int32, (V_SLAB - V_DIM, TK), 0) == 0).astype(bf16)

    @pl.when(st == 0)
    def _():
        tail_ref[...] = jnp.zeros_like(tail_ref)

    tail = tail_ref[...]
    for t in range(tm // TK):
        rows = slice(t * TK, (t + 1) * TK)
        h = (x_ref[0, rows, :] * (1.0 + scale) + shift).astype(bf16)
        proj = jnp.dot(h, w_in_ref[...], preferred_element_type=f32)
        cq = proj[:, :Q_LORA]
        ckv = proj[:, Q_LORA:Q_LORA + KV_LORA]
        krb = proj[:, Q_LORA + KV_LORA:Q_LORA + KV_LORA + HEAD_PAD]
        u = proj[:, Q_LORA + KV_LORA + HEAD_PAD:]
        cos_rows, sin_rows = _rope_slot_tables(pos_ref[0, :, rows].astype(f32), freq_ref[...])
        cos_t, sin_t = cos_rows.T, sin_rows.T

        cqn = _rmsnorm(cq, qg_ref[...]).astype(bf16)
        q_t = lax.dot_general(wuqt_ref[...], cqn, _NT, preferred_element_type=f32)
        for hh in range(N_HEADS):
            sl = slice(hh * HEAD_PAD, (hh + 1) * HEAD_PAD)
            blk = q_t[sl, :]
            blk = blk * cos_rows + pltpu.roll(blk, HEAD_PAD - QK_ROPE, 0) * sin_rows
            qt_ref[0, t, sl, :] = (blk * (SM_SCALE * LOG2_E)).astype(bf16)
        ckvn = _rmsnorm(ckv, kvg_ref[...]).astype(bf16)
        k_all = jnp.dot(ckvn, wuk_ref[...], preferred_element_type=f32)
        v_t = lax.dot_general(wuvt_ref[...], ckvn, _NT, preferred_element_type=f32).astype(bf16)
        for hh in range(N_HEADS):
            vt_ref[0, t, hh * V_SLAB:hh * V_SLAB + V_DIM, :] = v_t[hh * V_DIM:(hh + 1) * V_DIM, :]
            vt_ref[0, t, hh * V_SLAB + V_DIM:(hh + 1) * V_SLAB, :] = ones_rows
        k_rope = _apply_rope(krb, cos_t, sin_t)
        for hh in range(N_HEADS):
            sl = slice(hh * HEAD_PAD, (hh + 1) * HEAD_PAD)
            k_ref[0, rows, sl] = (k_all[:, sl] + k_rope).astype(bf16)

        ext = jnp.concatenate([tail, u], axis=0)
        tail = u[TK - POOL_HALO:, :]
        t_idx = st * tm + t * TK + lax.broadcasted_iota(jnp.int32, (TK, 1), 0)
        groups = []
        for gi, w in enumerate(POOL_WINDOWS):
            e = ext[:, gi * POOL_GROUP:(gi + 1) * POOL_GROUP]
            sh = 1
            while sh < w:
                e = e + pltpu.roll(e, sh, 0)
                sh *= 2
            cnt = jnp.minimum(t_idx + 1, w).astype(f32)
            groups.append(e[POOL_HALO:, :] / cnt - u[:, gi * POOL_GROUP:(gi + 1) * POOL_GROUP])
        pooled = jnp.concatenate(groups, axis=1).astype(bf16)
        yp = jnp.dot(pooled, wpool_ref[...], preferred_element_type=f32) + pbias_ref[...]
        yp_ref[0, rows, :] = (yp * pscale_ref[...]).astype(bf16)
    tail_ref[...] = tail


def _mix(x1, mod, pos, freq, w_in, qg, kvg, wuq, wuk, wuvt, wpool, pbias, pscale):
    B, S, D = x1.shape
    assert TQ == TK
    tm = TM_MIX
    hw = N_HEADS * HEAD_PAD
    vw = N_HEADS * V_SLAB

    def const(shape):
        return pl.BlockSpec(shape, lambda b, i: (0,) * len(shape))

    return pl.pallas_call(
        _mix_kernel,
        out_shape=(jax.ShapeDtypeStruct((B, S // TQ, hw, TQ), bf16),
                   jax.ShapeDtypeStruct((B, S, hw), bf16),
                   jax.ShapeDtypeStruct((B, S // TK, vw, TK), bf16),
                   jax.ShapeDtypeStruct((B, S, POOL_WIDTH), bf16)),
        grid=(B, S // tm),
        in_specs=[_row_spec(tm, D), _mod_spec(),
                  pl.BlockSpec((1, 1, tm), lambda b, i: (b, 0, i)), const(freq.shape),
                  const(w_in.shape), const(qg.shape), const(kvg.shape), const(wuq.shape),
                  const(wuk.shape), const(wuvt.shape), const(wpool.shape), const(pbias.shape),
                  const(pscale.shape)],
        out_specs=(pl.BlockSpec((1, tm // TQ, hw, TQ), lambda b, i: (b, i, 0, 0)),
                   _row_spec(tm, hw),
                   pl.BlockSpec((1, tm // TK, vw, TK), lambda b, i: (b, i, 0, 0)),
                   _row_spec(tm, POOL_WIDTH)),
        scratch_shapes=[pltpu.VMEM((POOL_HALO, POOL_WIDTH), f32)],
        compiler_params=pltpu.CompilerParams(
            dimension_semantics=("parallel", "arbitrary"), vmem_limit_bytes=VMEM_LIMIT),
        name="mix_proj",
    )(x1, mod, pos, freq, w_in, qg, kvg, wuq, wuk, wuvt, wpool, pbias, pscale)


def _attn_kernel(qt_ref, k_ref, vt_ref, o_ref, m_ref, acc_ref, s_a, s_b, mt_a, mt_b):
    base = pl.program_id(1) * Q_PER_STEP
    m_ref[...] = jnp.full_like(m_ref, -jnp.inf)
    acc_ref[...] = jnp.zeros_like(acc_ref)

    def qk_head(hh, sub, j, s_buf, mt_buf, masked):
        koff = pl.multiple_of(j * TK, TK)
        sl = slice(hh * HEAD_PAD, (hh + 1) * HEAD_PAD)
        s = jnp.dot(k_ref[0, pl.ds(koff, TK), sl], qt_ref[0, sub, sl, :],
                    preferred_element_type=f32)
        if masked:
            key_chunk = lax.broadcasted_iota(jnp.int32, (TK, TQ), 0) // CHUNK
            qry_chunk = lax.broadcasted_iota(jnp.int32, (TK, TQ), 1) // CHUNK
            s = jnp.where(key_chunk <= qry_chunk, s, -jnp.inf)
        s_buf[hh] = s
        mt_buf[hh] = jnp.max(s, axis=0, keepdims=True)

    def pv_head(hh, sub, j, s_buf, mt_buf):
        m_prev = m_ref[sub, hh]
        m_new = jnp.maximum(m_prev, mt_buf[hh])
        alpha = jnp.exp2(m_prev - m_new)
        p = jnp.exp2(s_buf[hh] - m_new).astype(bf16)
        v_t = vt_ref[0, j, hh * V_SLAB:(hh + 1) * V_SLAB, :]
        acc_ref[sub, hh] = alpha * acc_ref[sub, hh] + jnp.dot(v_t, p, preferred_element_type=f32)
        m_ref[sub, hh] = m_new

    def stage(qk=None, pv=None):
        for hh in range(N_HEADS):
            if qk is not None:
                qk_head(hh, *qk)
            if pv is not None:
                pv_head(hh, *pv)

    def key_tiles(sub, diag_tile, cur, nxt):
        def pair(u):
            stage(qk=(sub, 2 * u) + nxt + (False,),
                  pv=(sub, jnp.where(u == 0, diag_tile, 2 * u - 1)) + cur)
            stage(qk=(sub, 2 * u + 1) + cur + (False,), pv=(sub, 2 * u) + nxt)

        def body(v, carry):
            pair(2 * v)
            pair(2 * v + 1)
            return carry

        n_quads = base // 4
        lax.fori_loop(0, n_quads, body, 0)
        if (sub // 2) % 2 == 1:
            pair(2 * n_quads)
            return 4 * n_quads + 1
        return jnp.where(n_quads == 0, diag_tile, 4 * n_quads - 1)

    cur, other = (s_a, mt_a), (s_b, mt_b)
    stage(qk=(0, base) + cur + (True,))
    for sub in range(Q_PER_STEP):
        tile = base + sub
        pending = key_tiles(sub, tile, cur, other)
        if sub % 2 == 1:
            stage(qk=(sub, tile - 1) + other + (False,), pv=(sub, pending) + cur)
            last_tile, last_buf, free_buf = tile - 1, other, cur
        else:
            last_tile, last_buf, free_buf = pending, cur, other
        if sub + 1 < Q_PER_STEP:
            stage(qk=(sub + 1, tile + 1) + free_buf + (True,), pv=(sub, last_tile) + last_buf)
            cur, other = free_buf, last_buf
        else:
            stage(pv=(sub, last_tile) + last_buf)

    for sub in range(Q_PER_STEP):
        o_t = jnp.concatenate([acc_ref[sub, hh, :V_DIM, :] / acc_ref[sub, hh, V_DIM:V_DIM + 1, :]
                               for hh in range(N_HEADS)], axis=0)
        o_ref[0, sub * TQ:(sub + 1) * TQ, :] = o_t.T.astype(bf16)


def _attention(q_t, k, v_t):
    assert Q_PER_STEP == 4
    B, S, hw = k.shape
    vw = N_HEADS * V_DIM
    return pl.pallas_call(
        _attn_kernel,
        out_shape=jax.ShapeDtypeStruct((B, S, vw), bf16),
        grid=(B, S // (TQ * Q_PER_STEP)),
        in_specs=[pl.BlockSpec((1, Q_PER_STEP, hw, TQ), lambda b, i: (b, i, 0, 0)),
                  pl.BlockSpec((1, S, hw), lambda b, i: (b, 0, 0)),
                  pl.BlockSpec((1,) + v_t.shape[1:], lambda b, i: (b, 0, 0, 0))],
        out_specs=_row_spec(TQ * Q_PER_STEP, vw),
        scratch_shapes=[pltpu.VMEM((Q_PER_STEP, N_HEADS, 1, TQ), f32),
                        pltpu.VMEM((Q_PER_STEP, N_HEADS, V_SLAB, TQ), f32),
                        pltpu.VMEM((N_HEADS, TK, TQ), f32),
                        pltpu.VMEM((N_HEADS, TK, TQ), f32),
                        pltpu.VMEM((N_HEADS, 1, TQ), f32),
                        pltpu.VMEM((N_HEADS, 1, TQ), f32)],
        compiler_params=pltpu.CompilerParams(
            dimension_semantics=("parallel", "parallel"), vmem_limit_bytes=VMEM_LIMIT),
        name="attention",
    )(q_t, k, v_t)


def _rot_cols(w):
    half = w.shape[-1] // 2
    return jnp.concatenate([-w[..., half:], w[..., :half]], axis=-1)


def _head_slots(parts):
    k_dim = parts[0][0].shape[0]
    cols = []
    for blocks in parts:
        used = sum(b.shape[1] for b in blocks)
        cols.extend(blocks)
        if used < HEAD_PAD:
            cols.append(jnp.zeros((k_dim, HEAD_PAD - used), f32))
    return jnp.concatenate(cols, axis=1)


def kernel(x, c, positions, ln_in_g, ln_in_b, w_ada, b_ada, ffn1_w_in, ffn1_w_out, w_in, q_norm_g, kv_norm_g, w_uq, w_ukv, pool_w, pool_b, pool_scale, w_out, ffn2_w_in, ffn2_w_out, post_ln_g, post_ln_b):
    B, S, D = x.shape
    l = 0
    row = lambda a: a.reshape(1, -1)

    qh = QK_NOPE + QK_ROPE
    wuq = w_uq[l]
    wuq_pad = _head_slots([[wuq[:, h * qh:h * qh + QK_NOPE],
                            wuq[:, h * qh + QK_NOPE:(h + 1) * qh],
                            _rot_cols(wuq[:, h * qh + QK_NOPE:(h + 1) * qh])]
                           for h in range(N_HEADS)]).astype(bf16)
    kvh = QK_NOPE + V_DIM
    wukv = w_ukv[l]
    wuk_pad = _head_slots([[wukv[:, h * kvh:h * kvh + QK_NOPE]] for h in range(N_HEADS)]).astype(bf16)
    wuv_t = jnp.concatenate([wukv[:, h * kvh + QK_NOPE:(h + 1) * kvh] for h in range(N_HEADS)],
                            axis=1).T.astype(bf16)
    w_kr = w_in[l][:, Q_LORA + KV_LORA:Q_LORA + KV_LORA + QK_ROPE]
    w_in_pad = jnp.concatenate(
        [w_in[l][:, :Q_LORA + KV_LORA], jnp.zeros((D, QK_NOPE), f32), w_kr, _rot_cols(w_kr),
         w_in[l][:, Q_LORA + KV_LORA + QK_ROPE:]], axis=1).astype(bf16)
    wpool = jax.scipy.linalg.block_diag(*[pool_w[l, g] for g in range(len(POOL_WINDOWS))]).astype(bf16)
    wo = w_out[l]
    woa = wo[:N_HEADS * V_DIM].astype(bf16)
    wop = wo[N_HEADS * V_DIM:].astype(bf16)

    inv_freq = ROPE_THETA ** (-jnp.arange(0, QK_ROPE, 2, dtype=f32) / QK_ROPE)

    mod = _ada(c, w_ada[l], b_ada[l]).reshape(B, 9, D)

    x1 = _ffn1(x, mod, row(ln_in_g), row(ln_in_b), ffn1_w_in[l].astype(bf16),
               ffn1_w_out[l].astype(bf16), row(post_ln_g[l, 0]), row(post_ln_b[l, 0]))
    q_t, k, v_t, yp = _mix(x1, mod, positions.reshape(B, 1, S), inv_freq.reshape(-1, 1), w_in_pad,
                           row(q_norm_g[l]), row(kv_norm_g[l]),
                           wuq_pad.T, wuk_pad, wuv_t, wpool, row(pool_b[l]), row(pool_scale[l]))
    attn = _attention(q_t, k, v_t)
    return _ffn2(x1, attn, yp, mod, woa, wop, row(post_ln_g[l, 1]), row(post_ln_b[l, 1]),
                 ffn2_w_in[l].astype(bf16), ffn2_w_out[l].astype(bf16),
                 row(post_ln_g[l, 2]), row(post_ln_b[l, 2]))
```

```python
import jax
import jax.numpy as jnp
from jax import lax
from jax.experimental import pallas as pl
from jax.experimental.pallas import tpu as pltpu

f32 = jnp.float32
bf16 = jnp.bfloat16

D_MODEL = 1024
D_FF = 2816
N_HEADS = 8
QK_NOPE = 64
QK_ROPE = 32
V_DIM = 64
Q_LORA = 512
KV_LORA = 256
HEAD_PAD = 128
POOL_WINDOWS = (2, 4, 8, 16)
POOL_GROUP = 128
POOL_WIDTH = 512
POOL_HALO = 16
CHUNK = 64
ROPE_THETA = 10000.0
LN_EPS = 1e-5
RMS_EPS = 1e-6
ALPHA = 2.0 ** 0.25
SM_SCALE = (QK_NOPE + QK_ROPE) ** -0.5
LOG2_E = 1.4426950408889634
V_SLAB = 80
IN_PAD = Q_LORA + KV_LORA + HEAD_PAD + POOL_WIDTH

TM_FFN = 1024
TM_SUB = 256
TM_MIX = 1024
TQ = 256
TK = 256
Q_PER_STEP = 8
FF_CHUNKS = (1024, 1024, 768)
VMEM_LIMIT = 56 * 1024 * 1024

_NT = (((1,), (1,)), ((), ()))


def _layernorm(x, g, b):
    mu = jnp.mean(x, axis=-1, keepdims=True)
    xc = x - mu
    var = jnp.mean(xc * xc, axis=-1, keepdims=True)
    return xc * lax.rsqrt(var + LN_EPS) * g + b


def _rmsnorm(x, g):
    ms = jnp.mean(x * x, axis=-1, keepdims=True)
    return x * lax.rsqrt(ms + RMS_EPS) * g


def _swiglu(h, w_in_ref, w_out_ref):
    y = None
    off = 0
    for n in FF_CHUNKS:
        g = jnp.dot(h, w_in_ref[:, off:off + n], preferred_element_type=f32)
        u = jnp.dot(h, w_in_ref[:, D_FF + off:D_FF + off + n], preferred_element_type=f32)
        a = (g / (1.0 + jnp.exp(-g)) * u).astype(bf16)
        part = jnp.dot(a, w_out_ref[off:off + n, :], preferred_element_type=f32)
        y = part if y is None else y + part
        off += n
    return y


def _ada_kernel(c_ref, w_ref, b_ref, o_ref):
    c = c_ref[...]
    c_act = (c / (1.0 + jnp.exp(-c))).astype(bf16)
    o_ref[...] = jnp.dot(c_act, w_ref[...].astype(bf16), preferred_element_type=f32) + b_ref[...]


def _ada(c, w_ada, b_ada):
    B = c.shape[0]
    n_out = w_ada.shape[1]
    bn = D_MODEL
    return pl.pallas_call(
        _ada_kernel,
        out_shape=jax.ShapeDtypeStruct((B, n_out), f32),
        grid=(n_out // bn,),
        in_specs=[pl.BlockSpec((B, D_MODEL), lambda j: (0, 0)),
                  pl.BlockSpec((D_MODEL, bn), lambda j: (0, j)),
                  pl.BlockSpec((1, bn), lambda j: (0, j))],
        out_specs=pl.BlockSpec((B, bn), lambda j: (0, j)),
        name="ada",
    )(c, w_ada, b_ada.reshape(1, n_out))


def _ffn_block(x, mod_ref, first_row, w_in_ref, w_out_ref, pg_ref, pb_ref):
    shift, scale, gate = (mod_ref[0, first_row + r:first_row + r + 1, :] for r in range(3))
    h = (x * (1.0 + scale) + shift).astype(bf16)
    y = _swiglu(h, w_in_ref, w_out_ref)
    return _layernorm(ALPHA * x + 0.5 * gate * y, pg_ref[...], pb_ref[...])


def _ffn1_kernel(x_ref, mod_ref, lng_ref, lnb_ref, w_in_ref, w_out_ref, pg_ref, pb_ref, o_ref):
    for r0 in range(0, x_ref.shape[1], TM_SUB):
        rows = slice(r0, r0 + TM_SUB)
        x = _layernorm(x_ref[0, rows, :], lng_ref[...], lnb_ref[...])
        o_ref[0, rows, :] = _ffn_block(x, mod_ref, 0, w_in_ref, w_out_ref, pg_ref, pb_ref)


def _ffn2_kernel(x_ref, attn_ref, yp_ref, mod_ref, woa_ref, wop_ref, pg1_ref, pb1_ref,
                 w_in_ref, w_out_ref, pg2_ref, pb2_ref, o_ref):
    gate1 = mod_ref[0, 5:6, :]

    def mixer_residual(rows):
        y = (jnp.dot(attn_ref[0, rows, :], woa_ref[...], preferred_element_type=f32)
             + jnp.dot(yp_ref[0, rows, :], wop_ref[...], preferred_element_type=f32))
        return _layernorm(ALPHA * x_ref[0, rows, :] + gate1 * y, pg1_ref[...], pb1_ref[...])

    subs = [slice(r0, r0 + TM_SUB) for r0 in range(0, x_ref.shape[1], TM_SUB)]
    x_next = mixer_residual(subs[0])
    for n, rows in enumerate(subs):
        x = x_next
        if n + 1 < len(subs):
            x_next = mixer_residual(subs[n + 1])
        o_ref[0, rows, :] = _ffn_block(x, mod_ref, 6, w_in_ref, w_out_ref, pg2_ref, pb2_ref)


def _resident(shape):
    return pl.BlockSpec(shape, lambda b, i: (0,) * len(shape), pipeline_mode=pl.Buffered(1))


def _row_spec(tm, width):
    return pl.BlockSpec((1, tm, width), lambda b, i: (b, i, 0))


def _mod_spec():
    return pl.BlockSpec((1, 9, D_MODEL), lambda b, i: (b, 0, 0))


def _ffn1(x, mod, lng, lnb, w_in, w_out, pg, pb):
    B, S, D = x.shape
    vec = _resident((1, D))
    return pl.pallas_call(
        _ffn1_kernel,
        out_shape=jax.ShapeDtypeStruct((B, S, D), f32),
        grid=(B, S // TM_FFN),
        in_specs=[_row_spec(TM_FFN, D), _mod_spec(), vec, vec,
                  _resident(w_in.shape), _resident(w_out.shape), vec, vec],
        out_specs=_row_spec(TM_FFN, D),
        compiler_params=pltpu.CompilerParams(
            dimension_semantics=("parallel", "parallel"), vmem_limit_bytes=VMEM_LIMIT),
        name="ffn1",
    )(x, mod, lng, lnb, w_in, w_out, pg, pb)


def _ffn2(x1, attn, yp, mod, woa, wop, pg1, pb1, w_in, w_out, pg2, pb2):
    B, S, D = x1.shape
    vec = _resident((1, D))
    return pl.pallas_call(
        _ffn2_kernel,
        out_shape=jax.ShapeDtypeStruct((B, S, D), f32),
        grid=(B, S // TM_FFN),
        in_specs=[_row_spec(TM_FFN, D), _row_spec(TM_FFN, attn.shape[-1]),
                  _row_spec(TM_FFN, yp.shape[-1]), _mod_spec(),
                  _resident(woa.shape), _resident(wop.shape), vec, vec,
                  _resident(w_in.shape), _resident(w_out.shape), vec, vec],
        out_specs=_row_spec(TM_FFN, D),
        compiler_params=pltpu.CompilerParams(
            dimension_semantics=("parallel", "parallel"), vmem_limit_bytes=VMEM_LIMIT),
        name="ffn2",
    )(x1, attn, yp, mod, woa, wop, pg1, pb1, w_in, w_out, pg2, pb2)


def _apply_rope(t, cos_t, sin_t):
    return t * cos_t + pltpu.roll(t, HEAD_PAD - QK_ROPE, 1) * sin_t


def _rope_slot_tables(pos_row, freq_col):
    t = pos_row.shape[1]
    ang = freq_col * pos_row
    c, s = jnp.cos(ang), jnp.sin(ang)
    zeros = jnp.zeros((QK_ROPE, t), f32)
    cos_rows = jnp.concatenate([jnp.ones((QK_NOPE, t), f32), c, c, zeros], axis=0)
    sin_rows = jnp.concatenate([jnp.zeros((QK_NOPE, t), f32), s, s, zeros], axis=0)
    return cos_rows, sin_rows


def _mix_kernel(x_ref, mod_ref, pos_ref, freq_ref, w_in_ref, qg_ref, kvg_ref, wuqt_ref, wuk_ref,
                wuvt_ref, wpool_ref, pbias_ref, pscale_ref,
                qt_ref, k_ref, vt_ref, yp_ref, tail_ref):
    st = pl.program_id(1)
    tm = x_ref.shape[1]
    shift, scale = mod_ref[0, 3:4, :], mod_ref[0, 4:5, :]
    ones_rows = (lax.broadcasted_iota(jnp.int32, (V_SLAB - V_DIM, TK), 0) == 0).astype(bf16)

    @pl.when(st == 0)
    def _():
        tail_ref[...] = jnp.zeros_like(tail_ref)

    tail = tail_ref[...]
    for t in range(tm // TK):
        rows = slice(t * TK, (t + 1) * TK)
        h = (x_ref[0, rows, :] * (1.0 + scale) + shift).astype(bf16)
        proj = jnp.dot(h, w_in_ref[...], preferred_element_type=f32)
        cq = proj[:, :Q_LORA]
        ckv = proj[:, Q_LORA:Q_LORA + KV_LORA]
        krb = proj[:, Q_LORA + KV_LORA:Q_LORA + KV_LORA + HEAD_PAD]
        u = proj[:, Q_LORA + KV_LORA + HEAD_PAD:]
        cos_rows, sin_rows = _rope_slot_tables(pos_ref[0, :, rows].astype(f32), freq_ref[...])
        cos_t, sin_t = cos_rows.T, sin_rows.T

        cqn = _rmsnorm(cq, qg_ref[...]).astype(bf16)
        q_t = lax.dot_general(wuqt_ref[...], cqn, _NT, preferred_element_type=f32)
        for hh in range(N_HEADS):
            sl = slice(hh * HEAD_PAD, (hh + 1) * HEAD_PAD)
            blk = q_t[sl, :]
            blk = blk * cos_rows + pltpu.roll(blk, HEAD_PAD - QK_ROPE, 0) * sin_rows
            qt_ref[0, t, sl, :] = (blk * (SM_SCALE * LOG2_E)).astype(bf16)
        ckvn = _rmsnorm(ckv, kvg_ref[...]).astype(bf16)
        k_all = jnp.dot(ckvn, wuk_ref[...], preferred_element_type=f32)
        v_t = lax.dot_general(wuvt_ref[...], ckvn, _NT, preferred_element_type=f32).astype(bf16)
        for hh in range(N_HEADS):
            vt_ref[0, t, hh * V_SLAB:hh * V_SLAB + V_DIM, :] = v_t[hh * V_DIM:(hh + 1) * V_DIM, :]
            vt_ref[0, t, hh * V_SLAB + V_DIM:(hh + 1) * V_SLAB, :] = ones_rows
        k_rope = _apply_rope(krb, cos_t, sin_t)
        for hh in range(N_HEADS):
            sl = slice(hh * HEAD_PAD, (hh + 1) * HEAD_PAD)
            k_ref[0, rows, sl] = (k_all[:, sl] + k_rope).astype(bf16)

        ext = jnp.concatenate([tail, u], axis=0)
        tail = u[TK - POOL_HALO:, :]
        t_idx = st * tm + t * TK + lax.broadcasted_iota(jnp.int32, (TK, 1), 0)
        groups = []
        for gi, w in enumerate(POOL_WINDOWS):
            e = ext[:, gi * POOL_GROUP:(gi + 1) * POOL_GROUP]
            sh = 1
            while sh < w:
                e = e + pltpu.roll(e, sh, 0)
                sh *= 2
            cnt = jnp.minimum(t_idx + 1, w).astype(f32)
            groups.append(e[POOL_HALO:, :] / cnt - u[:, gi * POOL_GROUP:(gi + 1) * POOL_GROUP])
        pooled = jnp.concatenate(groups, axis=1).astype(bf16)
        yp = jnp.dot(pooled, wpool_ref[...], preferred_element_type=f32) + pbias_ref[...]
        yp_ref[0, rows, :] = (yp * pscale_ref[...]).astype(bf16)
    tail_ref[...] = tail


def _mix(x1, mod, pos, freq, w_in, qg, kvg, wuq, wuk, wuvt, wpool, pbias, pscale):
    B, S, D = x1.shape
    assert TQ == TK
    tm = TM_MIX
    hw = N_HEADS * HEAD_PAD
    vw = N_HEADS * V_SLAB

    def const(shape):
        return pl.BlockSpec(shape, lambda b, i: (0,) * len(shape))

    return pl.pallas_call(
        _mix_kernel,
        out_shape=(jax.ShapeDtypeStruct((B, S // TQ, hw, TQ), bf16),
                   jax.ShapeDtypeStruct((B, S, hw), bf16),
                   jax.ShapeDtypeStruct((B, S // TK, vw, TK), bf16),
                   jax.ShapeDtypeStruct((B, S, POOL_WIDTH), bf16)),
        grid=(B, S // tm),
        in_specs=[_row_spec(tm, D), _mod_spec(),
                  pl.BlockSpec((1, 1, tm), lambda b, i: (b, 0, i)), const(freq.shape),
                  const(w_in.shape), const(qg.shape), const(kvg.shape), const(wuq.shape),
                  const(wuk.shape), const(wuvt.shape), const(wpool.shape), const(pbias.shape),
                  const(pscale.shape)],
        out_specs=(pl.BlockSpec((1, tm // TQ, hw, TQ), lambda b, i: (b, i, 0, 0)),
                   _row_spec(tm, hw),
                   pl.BlockSpec((1, tm // TK, vw, TK), lambda b, i: (b, i, 0, 0)),
                   _row_spec(tm, POOL_WIDTH)),
        scratch_shapes=[pltpu.VMEM((POOL_HALO, POOL_WIDTH), f32)],
        compiler_params=pltpu.CompilerParams(
            dimension_semantics=("parallel", "arbitrary"), vmem_limit_bytes=VMEM_LIMIT),
        name="mix_proj",
    )(x1, mod, pos, freq, w_in, qg, kvg, wuq, wuk, wuvt, wpool, pbias, pscale)


def _attn_kernel(qt_ref, k_ref, vt_ref, o_ref, m_ref, acc_ref, s_a, s_b, mt_a, mt_b):
    base = pl.program_id(1) * Q_PER_STEP
    m_ref[...] = jnp.full_like(m_ref, -jnp.inf)
    acc_ref[...] = jnp.zeros_like(acc_ref)

    def qk_head(hh, sub, j, s_buf, mt_buf, masked):
        koff = pl.multiple_of(j * TK, TK)
        sl = slice(hh * HEAD_PAD, (hh + 1) * HEAD_PAD)
        s = jnp.dot(k_ref[0, pl.ds(koff, TK), sl], qt_ref[0, sub, sl, :],
                    preferred_element_type=f32)
        if masked:
            key_chunk = lax.broadcasted_iota(jnp.int32, (TK, TQ), 0) // CHUNK
            qry_chunk = lax.broadcasted_iota(jnp.int32, (TK, TQ), 1) // CHUNK
            s = jnp.where(key_chunk <= qry_chunk, s, -jnp.inf)
        s_buf[hh] = s
        mt_buf[hh] = jnp.max(s, axis=0, keepdims=True)

    def pv_head(hh, sub, j, s_buf, mt_buf):
        m_prev = m_ref[sub, hh]
        m_new = jnp.maximum(m_prev, mt_buf[hh])
        alpha = jnp.exp2(m_prev - m_new)
        p = jnp.exp2(s_buf[hh] - m_new).astype(bf16)
        v_t = vt_ref[0, j, hh * V_SLAB:(hh + 1) * V_SLAB, :]
        acc_ref[sub, hh] = alpha * acc_ref[sub, hh] + jnp.dot(v_t, p, preferred_element_type=f32)
        m_ref[sub, hh] = m_new

    def stage(qk=None, pv=None):
        for hh in range(N_HEADS):
            if qk is not None:
                qk_head(hh, *qk)
            if pv is not None:
                pv_head(hh, *pv)

    def key_tiles(sub, diag_tile, cur, nxt):
        def pair(u):
            stage(qk=(sub, 2 * u) + nxt + (False,),
                  pv=(sub, jnp.where(u == 0, diag_tile, 2 * u - 1)) + cur)
            stage(qk=(sub, 2 * u + 1) + cur + (False,), pv=(sub, 2 * u) + nxt)

        def body(v, carry):
            pair(2 * v)
            pair(2 * v + 1)
            return carry

        n_quads = base // 4 + sub // 4
        lax.fori_loop(0, n_quads, body, 0)
        if (sub // 2) % 2 == 1:
            pair(2 * n_quads)
            return 4 * n_quads + 1
        return jnp.where(n_quads == 0, diag_tile, 4 * n_quads - 1)

    cur, other = (s_a, mt_a), (s_b, mt_b)
    stage(qk=(0, base) + cur + (True,))
    for sub in range(Q_PER_STEP):
        tile = base + sub
        pending = key_tiles(sub, tile, cur, other)
        if sub % 2 == 1:
            stage(qk=(sub, tile - 1) + other + (False,), pv=(sub, pending) + cur)
            last_tile, last_buf, free_buf = tile - 1, other, cur
        else:
            last_tile, last_buf, free_buf = pending, cur, other
        if sub + 1 < Q_PER_STEP:
            stage(qk=(sub + 1, tile + 1) + free_buf + (True,), pv=(sub, last_tile) + last_buf)
            cur, other = free_buf, last_buf
        else:
            stage(pv=(sub, last_tile) + last_buf)

    for sub in range(Q_PER_STEP):
        o_t = jnp.concatenate([acc_ref[sub, hh, :V_DIM, :] / acc_ref[sub, hh, V_DIM:V_DIM + 1, :]
                               for hh in range(N_HEADS)], axis=0)
        o_ref[0, sub * TQ:(sub + 1) * TQ, :] = o_t.T.astype(bf16)


def _attention(q_t, k, v_t):
    assert Q_PER_STEP % 4 == 0
    B, S, hw = k.shape
    vw = N_HEADS * V_DIM
    return pl.pallas_call(
        _attn_kernel,
        out_shape=jax.ShapeDtypeStruct((B, S, vw), bf16),
        grid=(B, S // (TQ * Q_PER_STEP)),
        in_specs=[pl.BlockSpec((1, Q_PER_STEP, hw, TQ), lambda b, i: (b, i, 0, 0)),
                  pl.BlockSpec((1, S, hw), lambda b, i: (b, 0, 0)),
                  pl.BlockSpec((1,) + v_t.shape[1:], lambda b, i: (b, 0, 0, 0))],
        out_specs=_row_spec(TQ * Q_PER_STEP, vw),
        scratch_shapes=[pltpu.VMEM((Q_PER_STEP, N_HEADS, 1, TQ), f32),
                        pltpu.VMEM((Q_PER_STEP, N_HEADS, V_SLAB, TQ), f32),
                        pltpu.VMEM((N_HEADS, TK, TQ), f32),
                        pltpu.VMEM((N_HEADS, TK, TQ), f32),
                        pltpu.VMEM((N_HEADS, 1, TQ), f32),
                        pltpu.VMEM((N_HEADS, 1, TQ), f32)],
        compiler_params=pltpu.CompilerParams(
            dimension_semantics=("parallel", "parallel"), vmem_limit_bytes=VMEM_LIMIT),
        name="attention",
    )(q_t, k, v_t)


def _rot_cols(w):
    half = w.shape[-1] // 2
    return jnp.concatenate([-w[..., half:], w[..., :half]], axis=-1)


def _head_slots(parts):
    k_dim = parts[0][0].shape[0]
    cols = []
    for blocks in parts:
        used = sum(b.shape[1] for b in blocks)
        cols.extend(blocks)
        if used < HEAD_PAD:
            cols.append(jnp.zeros((k_dim, HEAD_PAD - used), f32))
    return jnp.concatenate(cols, axis=1)


def kernel(x, c, positions, ln_in_g, ln_in_b, w_ada, b_ada, ffn1_w_in, ffn1_w_out, w_in, q_norm_g, kv_norm_g, w_uq, w_ukv, pool_w, pool_b, pool_scale, w_out, ffn2_w_in, ffn2_w_out, post_ln_g, post_ln_b):
    B, S, D = x.shape
    l = 0
    row = lambda a: a.reshape(1, -1)

    qh = QK_NOPE + QK_ROPE
    wuq = w_uq[l]
    wuq_pad = _head_slots([[wuq[:, h * qh:h * qh + QK_NOPE],
                            wuq[:, h * qh + QK_NOPE:(h + 1) * qh],
                            _rot_cols(wuq[:, h * qh + QK_NOPE:(h + 1) * qh])]
                           for h in range(N_HEADS)]).astype(bf16)
    kvh = QK_NOPE + V_DIM
    wukv = w_ukv[l]
    wuk_pad = _head_slots([[wukv[:, h * kvh:h * kvh + QK_NOPE]] for h in range(N_HEADS)]).astype(bf16)
    wuv_t = jnp.concatenate([wukv[:, h * kvh + QK_NOPE:(h + 1) * kvh] for h in range(N_HEADS)],
                            axis=1).T.astype(bf16)
    w_kr = w_in[l][:, Q_LORA + KV_LORA:Q_LORA + KV_LORA + QK_ROPE]
    w_in_pad = jnp.concatenate(
        [w_in[l][:, :Q_LORA + KV_LORA], jnp.zeros((D, QK_NOPE), f32), w_kr, _rot_cols(w_kr),
         w_in[l][:, Q_LORA + KV_LORA + QK_ROPE:]], axis=1).astype(bf16)
    wpool = jax.scipy.linalg.block_diag(*[pool_w[l, g] for g in range(len(POOL_WINDOWS))]).astype(bf16)
    wo = w_out[l]
    woa = wo[:N_HEADS * V_DIM].astype(bf16)
    wop = wo[N_HEADS * V_DIM:].astype(bf16)

    inv_freq = ROPE_THETA ** (-jnp.arange(0, QK_ROPE, 2, dtype=f32) / QK_ROPE)

    mod = _ada(c, w_ada[l], b_ada[l]).reshape(B, 9, D)

    x1 = _ffn1(x, mod, row(ln_in_g), row(ln_in_b), ffn1_w_in[l].astype(bf16),
               ffn1_w_out[l].astype(bf16), row(post_ln_g[l, 0]), row(post_ln_b[l, 0]))
    q_t, k, v_t, yp = _mix(x1, mod, positions.reshape(B, 1, S), inv_freq.reshape(-1, 1), w_in_pad,
                           row(q_norm_g[l]), row(kv_norm_g[l]),
                           wuq_pad.T, wuk_pad, wuv_t, wpool, row(pool_b[l]), row(pool_scale[l]))
    attn = _attention(q_t, k, v_t)
    return _ffn2(x1, attn, yp, mod, woa, wop, row(post_ln_g[l, 1]), row(post_ln_b[l, 1]),
                 ffn2_w_in[l].astype(bf16), ffn2_w_out[l].astype(bf16),
                 row(post_ln_g[l, 2]), row(post_ln_b[l, 2]))
```
